```python
import jax, jax.numpy as jnp
from jax import lax
import numpy as np

D_MODEL = 2048
BATCH = 16
SEQ = 2048
DEPTH = 2

WIDTH_POOL = D_MODEL // 4
WIDTH_CONV = (3 * D_MODEL) // 8
WIDTH_SGU = D_MODEL - WIDTH_POOL - WIDTH_CONV
POOL_WINDOWS = (2, 4, 8, 16)
N_POOL_GROUPS = len(POOL_WINDOWS)
POOL_GROUP_DIM = WIDTH_POOL // N_POOL_GROUPS
CONV_WIDTH = 31
N_SGU_HEADS = 6
SGU_HEAD_DIM = WIDTH_SGU // N_SGU_HEADS
CHUNK = 128
D_FF = 4 * D_MODEL
IN_WIDTH = WIDTH_POOL + 2 * WIDTH_CONV + 2 * WIDTH_SGU
N_MOD = 6
LN_EPS = 1e-5
ALPHA = (2.0 * DEPTH) ** 0.25
BETA = (8.0 * DEPTH) ** -0.25
MOD_INIT = 0.5

kernel_name = 'hybrid_pool_conv_sgu_deepnorm_block'


def _layer_norm(x, g, b):
    xf = x.astype(jnp.float32)
    mu = jnp.mean(xf, axis=-1, keepdims=True)
    var = jnp.mean(jnp.square(xf - mu), axis=-1, keepdims=True)
    y = (xf - mu) * lax.rsqrt(var + LN_EPS)
    return (y * g.astype(jnp.float32) + b.astype(jnp.float32)).astype(x.dtype)


def _causal_mean_minus_self(a, window):
    S = a.shape[1]
    af = a.astype(jnp.float32)
    cs = jnp.cumsum(af, axis=1)
    lagged = jnp.pad(cs, ((0, 0), (window, 0), (0, 0)))[:, :S]
    count = jnp.minimum(jnp.arange(1, S + 1, dtype=jnp.float32), float(window))
    mean = (cs - lagged) / count[None, :, None]
    return (mean - af).astype(a.dtype)


def _pool_mixer(a, w_grp, ls):
    B, S, _ = a.shape
    a = a.reshape(B, S, N_POOL_GROUPS, POOL_GROUP_DIM)
    pooled = jnp.stack([_causal_mean_minus_self(a[:, :, g], w) for g, w in enumerate(POOL_WINDOWS)], axis=2)
    y = jnp.einsum('bsgc,gcd->bsgd', pooled, w_grp)
    return y.reshape(B, S, WIDTH_POOL) * ls


def _conv_module(val, gate, conv_w, conv_b, ln_g, ln_b):
    h = val * jax.nn.sigmoid(gate)
    h = lax.conv_general_dilated(
        h, conv_w[:, None, :].astype(h.dtype), window_strides=(1,),
        padding=[(CONV_WIDTH - 1, 0)],
        dimension_numbers=('NWC', 'WIO', 'NWC'),
        feature_group_count=WIDTH_CONV) + conv_b
    h = _layer_norm(h, ln_g, ln_b)
    return jax.nn.silu(h)


def _sgu(z, w_s, b_s, ln_g, ln_b):
    z = jax.nn.gelu(z, approximate=False)
    u, v = jnp.split(z, 2, axis=-1)
    v = _layer_norm(v, ln_g, ln_b)
    B, S, _ = v.shape
    n_chunks = S // CHUNK
    v = v.reshape(B, n_chunks, CHUNK, N_SGU_HEADS, SGU_HEAD_DIM)
    mask = jnp.tril(jnp.ones((CHUNK, CHUNK), dtype=w_s.dtype))
    v = jnp.einsum('hts,bnshd->bnthd', w_s * mask, v) + b_s.T[:, :, None]
    return u * v.reshape(B, S, WIDTH_SGU)


def setup_inputs(seed: int = 0) -> dict:
    key = jax.random.key(seed)
    ks = jax.random.split(key, 24)
    f32 = jnp.float32
    nrm = lambda k, shape, s: jax.random.normal(k, shape, f32) * s
    L, D = DEPTH, D_MODEL
    return {
        'x': nrm(ks[0], (BATCH, SEQ, D), 1.0),
        'c': nrm(ks[1], (BATCH, D), 1.0),
        'w_mod': nrm(ks[2], (L, D, N_MOD * D), MOD_INIT * D ** -0.5),
        'b_mod': nrm(ks[3], (L, N_MOD * D), 0.02),
        'w_in': nrm(ks[4], (L, D, IN_WIDTH), D ** -0.5),
        'w_pool': nrm(ks[5], (L, N_POOL_GROUPS, POOL_GROUP_DIM, POOL_GROUP_DIM), POOL_GROUP_DIM ** -0.5),
        'ls_pool': 1.0 + nrm(ks[6], (L, WIDTH_POOL), 0.02),
        'conv_w': nrm(ks[7], (L, CONV_WIDTH, WIDTH_CONV), CONV_WIDTH ** -0.5),
        'conv_b': nrm(ks[8], (L, WIDTH_CONV), 0.02),
        'ln_conv_g': 1.0 + nrm(ks[9], (L, WIDTH_CONV), 0.02),
        'ln_conv_b': nrm(ks[10], (L, WIDTH_CONV), 0.02),
        'w_sgu': nrm(ks[11], (L, N_SGU_HEADS, CHUNK, CHUNK), CHUNK ** -0.5),
        'b_sgu': 1.0 + nrm(ks[12], (L, N_SGU_HEADS, CHUNK), 0.02),
        'ln_sgu_g': 1.0 + nrm(ks[13], (L, WIDTH_SGU), 0.02),
        'ln_sgu_b': nrm(ks[14], (L, WIDTH_SGU), 0.02),
        'w_out': nrm(ks[15], (L, D, D), BETA * D ** -0.5),
        'ln_mix_g': 1.0 + nrm(ks[16], (L, D), 0.02),
        'ln_mix_b': nrm(ks[17], (L, D), 0.02),
        'w_ff1': nrm(ks[18], (L, D, D_FF), D ** -0.5),
        'w_ff2': nrm(ks[19], (L, D_FF, D), BETA * D_FF ** -0.5),
        'ln_ff_g': 1.0 + nrm(ks[20], (L, D), 0.02),
        'ln_ff_b': nrm(ks[21], (L, D), 0.02),
    }


def reference(x, c, w_mod, b_mod, w_in, w_pool, ls_pool, conv_w, conv_b, ln_conv_g, ln_conv_b,
              w_sgu, b_sgu, ln_sgu_g, ln_sgu_b, w_out, ln_mix_g, ln_mix_b,
              w_ff1, w_ff2, ln_ff_g, ln_ff_b):
    c_act = jax.nn.silu(c)
    split_at = [WIDTH_POOL, WIDTH_POOL + WIDTH_CONV, WIDTH_POOL + 2 * WIDTH_CONV]
    for l in range(DEPTH):
        mod = c_act @ w_mod[l] + b_mod[l]
        sh_m, sc_m, g_m, sh_f, sc_f, g_f = [m[:, None, :] for m in jnp.split(mod, N_MOD, axis=-1)]

        h = x * (1.0 + sc_m) + sh_m
        p = h @ w_in[l]
        a, val, gate, z = jnp.split(p, split_at, axis=-1)
        y_pool = _pool_mixer(a, w_pool[l], ls_pool[l])
        y_conv = _conv_module(val, gate, conv_w[l], conv_b[l], ln_conv_g[l], ln_conv_b[l])
        y_sgu = _sgu(z, w_sgu[l], b_sgu[l], ln_sgu_g[l], ln_sgu_b[l])
        y = jnp.concatenate([y_pool, y_conv, y_sgu], axis=-1) @ w_out[l]
        x = _layer_norm(ALPHA * x + g_m * y, ln_mix_g[l], ln_mix_b[l])

        h = x * (1.0 + sc_f) + sh_f
        f = jnp.square(jax.nn.relu(h @ w_ff1[l])) @ w_ff2[l]
        x = _layer_norm(ALPHA * x + g_f * f, ln_ff_g[l], ln_ff_b[l])
    return x
```

```python
import functools

import jax
import jax.numpy as jnp
from jax import lax
from jax.experimental import pallas as pl
from jax.experimental.pallas import tpu as pltpu

D_MODEL = 2048
DEPTH = 2
WIDTH_POOL = D_MODEL // 4
WIDTH_CONV = (3 * D_MODEL) // 8
WIDTH_SGU = D_MODEL - WIDTH_POOL - WIDTH_CONV
POOL_WINDOWS = (2, 4, 8, 16)
POOL_GROUP_DIM = WIDTH_POOL // len(POOL_WINDOWS)
CONV_WIDTH = 31
N_SGU_HEADS = 6
SGU_HEAD_DIM = WIDTH_SGU // N_SGU_HEADS
CHUNK = 128
D_FF = 4 * D_MODEL
IN_WIDTH = WIDTH_POOL + 2 * WIDTH_CONV + 2 * WIDTH_SGU
N_MOD = 6
LN_EPS = 1e-5
ALPHA = (2.0 * DEPTH) ** 0.25

OFF_VAL = WIDTH_POOL
OFF_GATE = OFF_VAL + WIDTH_CONV
OFF_Z = OFF_GATE + WIDTH_CONV

V7X_VMEM_BYTES = 64 * 1024 * 1024
VMEM_LIMIT_BYTES = V7X_VMEM_BYTES - 8 * 1024 * 1024

HALO = 32
MIX_ROWS = 256
CONV_ROWS = 32
FFN_ROWS = 1024
FFN_COLS = 512
MOD_COLS = 1024

F32 = jnp.float32
BF16 = jnp.bfloat16


def _layer_norm(x, g, b):
    mu = jnp.mean(x, axis=-1, keepdims=True)
    xc = x - mu
    var = jnp.mean(xc * xc, axis=-1, keepdims=True)
    return xc * lax.rsqrt(var + LN_EPS) * g + b


def _mod_kernel(c_ref, w_ref, b_ref, o_ref):
    c = c_ref[...]
    c_act = (c * jax.nn.sigmoid(c)).astype(BF16)
    w = w_ref[...].astype(BF16)
    o_ref[...] = jnp.dot(c_act, w, preferred_element_type=F32) + b_ref[...]


def _mod_call(c, w_mod, b_mod):
    depth, d, n = w_mod.shape
    batch = c.shape[0]
    return pl.pallas_call(
        _mod_kernel,
        grid=(depth, n // MOD_COLS),
        in_specs=[
            pl.BlockSpec((batch, d), lambda l, j: (0, 0)),
            pl.BlockSpec((None, d, MOD_COLS), lambda l, j: (l, 0, j)),
            pl.BlockSpec((None, 1, MOD_COLS), lambda l, j: (l, 0, j)),
        ],
        out_specs=pl.BlockSpec((None, batch, MOD_COLS), lambda l, j: (l, 0, j)),
        out_shape=jax.ShapeDtypeStruct((depth, batch, n), F32),
        compiler_params=pltpu.CompilerParams(
            dimension_semantics=("arbitrary", "arbitrary"), vmem_limit_bytes=VMEM_LIMIT_BYTES),
        name="mod",
    )(c, w_mod, b_mod.reshape(depth, 1, n))


def _mixer_kernel(x_ref, mod_ref, w_in_ref, w_pool_ref, ls_pool_ref, conv_w_ref, conv_b_ref,
                  ln_conv_g_ref, ln_conv_b_ref, w_sgu_ref, b_sgu_ref, ln_sgu_g_ref, ln_sgu_b_ref,
                  w_out_ref, ln_g_ref, ln_b_ref, o_ref, a_hist, g_hist, ycat):
    ts = x_ref.shape[0]
    j = pl.program_id(1)

    @pl.when(j == 0)
    def _():
        a_hist[0:HALO, :] = jnp.zeros((HALO, WIDTH_POOL), F32)
        g_hist[0:HALO, :] = jnp.zeros((HALO, WIDTH_CONV), F32)

    shift = mod_ref[0:1, :]
    scale = mod_ref[1:2, :]
    gate = mod_ref[2:3, :]

    x = x_ref[...]
    h = (x * (1.0 + scale) + shift).astype(BF16)

    a = jnp.dot(h, w_in_ref[:, 0:WIDTH_POOL], preferred_element_type=F32)
    a_hist[HALO:HALO + ts, :] = a
    pos = j * ts + lax.broadcasted_iota(jnp.int32, (ts, 1), 0) + 1
    for gi, win in enumerate(POOL_WINDOWS):
        lanes = slice(gi * POOL_GROUP_DIM, (gi + 1) * POOL_GROUP_DIM)
        cur = a_hist[HALO:HALO + ts, lanes]
        wsum = cur
        for back in range(1, win):
            wsum = wsum + a_hist[HALO - back:HALO - back + ts, lanes]
        count = jnp.minimum(pos, win).astype(F32)
        pooled = (wsum / count - cur).astype(BF16)
        y = jnp.dot(pooled, w_pool_ref[gi], preferred_element_type=F32)
        ycat[:, lanes] = (y * ls_pool_ref[:, lanes]).astype(BF16)
    a_hist[0:HALO, :] = a_hist[ts:ts + HALO, :]

    val = jnp.dot(h, w_in_ref[:, OFF_VAL:OFF_GATE], preferred_element_type=F32)
    gt = jnp.dot(h, w_in_ref[:, OFF_GATE:OFF_Z], preferred_element_type=F32)
    g_hist[HALO:HALO + ts, :] = val * jax.nn.sigmoid(gt)
    for r0 in range(0, ts, CONV_ROWS):
        acc = jnp.broadcast_to(conv_b_ref[...], (CONV_ROWS, WIDTH_CONV))
        for k in range(CONV_WIDTH):
            start = HALO + r0 - (CONV_WIDTH - 1) + k
            acc = acc + conv_w_ref[k:k + 1, :] * g_hist[start:start + CONV_ROWS, :]
        hn = _layer_norm(acc, ln_conv_g_ref[...], ln_conv_b_ref[...])
        ycat[r0:r0 + CONV_ROWS, OFF_VAL:OFF_GATE] = (hn * jax.nn.sigmoid(hn)).astype(BF16)
    g_hist[0:HALO, :] = g_hist[ts:ts + HALO, :]

    z = jnp.dot(h, w_in_ref[:, OFF_Z:IN_WIDTH], preferred_element_type=F32)
    z = 0.5 * z * (1.0 + lax.erf(z * (2.0 ** -0.5)))
    u = z[:, 0:WIDTH_SGU]
    v = _layer_norm(z[:, WIDTH_SGU:], ln_sgu_g_ref[...], ln_sgu_b_ref[...]).astype(BF16)
    n_chunks = ts // CHUNK
    row = lax.broadcasted_iota(jnp.int32, (CHUNK, CHUNK), 0)
    col = lax.broadcasted_iota(jnp.int32, (CHUNK, CHUNK), 1)
    causal = col <= row
    for hd in range(N_SGU_HEADS):
        lanes = slice(hd * SGU_HEAD_DIM, (hd + 1) * SGU_HEAD_DIM)
        w_h = jnp.where(causal, w_sgu_ref[hd], 0.0).astype(BF16)
        v_h = jnp.concatenate([v[n * CHUNK:(n + 1) * CHUNK, lanes] for n in range(n_chunks)], axis=1)
        m_h = jnp.dot(w_h, v_h, preferred_element_type=F32)
        for n in range(n_chunks):
            rows = slice(n * CHUNK, (n + 1) * CHUNK)
            mixed = m_h[:, n * SGU_HEAD_DIM:(n + 1) * SGU_HEAD_DIM] + b_sgu_ref[:, lanes]
            ycat[rows, OFF_GATE + hd * SGU_HEAD_DIM:OFF_GATE + (hd + 1) * SGU_HEAD_DIM] = (
                u[rows, lanes] * mixed).astype(BF16)

    y = jnp.dot(ycat[...], w_out_ref[...], preferred_element_type=F32)
    o_ref[...] = _layer_norm(ALPHA * x + gate * y, ln_g_ref[...], ln_b_ref[...])


def _mixer_call(x, mod, w_in, w_pool, ls_pool, conv_w, conv_b, ln_conv_g, ln_conv_b,
                w_sgu, b_sgu_rows, ln_sgu_g, ln_sgu_b, w_out, ln_g, ln_b):
    batch, seq, d = x.shape
    ts = MIX_ROWS
    const2 = lambda b, j: (0, 0)
    const3 = lambda b, j: (0, 0, 0)
    resident = pl.Buffered(1)
    row = lambda v: v.reshape(1, -1)
    return pl.pallas_call(
        _mixer_kernel,
        grid=(batch, seq // ts),
        in_specs=[
            pl.BlockSpec((None, ts, d), lambda b, j: (b, j, 0)),
            pl.BlockSpec((None, N_MOD, d), lambda b, j: (b, 0, 0)),
            pl.BlockSpec(w_in.shape, const2, pipeline_mode=resident),
            pl.BlockSpec(w_pool.shape, const3),
            pl.BlockSpec((1, WIDTH_POOL), const2),
            pl.BlockSpec(conv_w.shape, const2),
            pl.BlockSpec((1, WIDTH_CONV), const2),
            pl.BlockSpec((1, WIDTH_CONV), const2),
            pl.BlockSpec((1, WIDTH_CONV), const2),
            pl.BlockSpec(w_sgu.shape, const3),
            pl.BlockSpec(b_sgu_rows.shape, const2),
            pl.BlockSpec((1, WIDTH_SGU), const2),
            pl.BlockSpec((1, WIDTH_SGU), const2),
            pl.BlockSpec(w_out.shape, const2, pipeline_mode=resident),
            pl.BlockSpec((1, d), const2),
            pl.BlockSpec((1, d), const2),
        ],
        out_specs=pl.BlockSpec((None, ts, d), lambda b, j: (b, j, 0)),
        out_shape=jax.ShapeDtypeStruct(x.shape, F32),
        scratch_shapes=[
            pltpu.VMEM((HALO + ts, WIDTH_POOL), F32),
            pltpu.VMEM((HALO + ts, WIDTH_CONV), F32),
            pltpu.VMEM((ts, d), BF16),
        ],
        compiler_params=pltpu.CompilerParams(
            dimension_semantics=("arbitrary", "arbitrary"), vmem_limit_bytes=VMEM_LIMIT_BYTES),
        name="mixer",
    )(x, mod, w_in, w_pool, row(ls_pool), conv_w, row(conv_b), row(ln_conv_g), row(ln_conv_b),
      w_sgu, b_sgu_rows, row(ln_sgu_g), row(ln_sgu_b), w_out, row(ln_g), row(ln_b))


def _ffn_kernel(x_ref, mod_ref, w1_ref, w2_ref, ln_g_ref, ln_b_ref, o_ref, h_ref):
    k = pl.program_id(1)

    @pl.when(k == 0)
    def _():
        shift = mod_ref[3:4, :]
        scale = mod_ref[4:5, :]
        h_ref[...] = (x_ref[...] * (1.0 + scale) + shift).astype(BF16)

    a = jnp.dot(h_ref[...], w1_ref[...], preferred_element_type=F32)
    a = jnp.maximum(a, 0.0)
    part = jnp.dot((a * a).astype(BF16), w2_ref[...], preferred_element_type=F32)

    @pl.when(k == 0)
    def _():
        o_ref[...] = part

    @pl.when(k > 0)
    def _():
        o_ref[...] += part

    @pl.when(k == pl.num_programs(1) - 1)
    def _():
        gate = mod_ref[5:6, :]
        o_ref[...] = _layer_norm(ALPHA * x_ref[...] + gate * o_ref[...], ln_g_ref[...], ln_b_ref[...])


def _ffn_call(x, mod, w1, w2, ln_g, ln_b):
    batch, seq, d = x.shape
    d_ff = w1.shape[1]
    tm = FFN_ROWS
    tiles_per_seq = seq // tm
    x_map = lambda i, k: (i // tiles_per_seq, i % tiles_per_seq, 0)
    return pl.pallas_call(
        _ffn_kernel,
        grid=(batch * tiles_per_seq, d_ff // FFN_COLS),
        in_specs=[
            pl.BlockSpec((None, tm, d), x_map, pipeline_mode=pl.Buffered(1)),
            pl.BlockSpec((None, N_MOD, d), lambda i, k: (i // tiles_per_seq, 0, 0)),
            pl.BlockSpec((d, FFN_COLS), lambda i, k: (0, k)),
            pl.BlockSpec((FFN_COLS, d), lambda i, k: (k, 0)),
            pl.BlockSpec((1, d), lambda i, k: (0, 0)),
            pl.BlockSpec((1, d), lambda i, k: (0, 0)),
        ],
        out_specs=pl.BlockSpec((None, tm, d), x_map),
        out_shape=jax.ShapeDtypeStruct(x.shape, F32),
        scratch_shapes=[pltpu.VMEM((tm, d), BF16)],
        compiler_params=pltpu.CompilerParams(
            dimension_semantics=("arbitrary", "arbitrary"), vmem_limit_bytes=VMEM_LIMIT_BYTES),
        name="ffn",
    )(x, mod, w1, w2, ln_g.reshape(1, d), ln_b.reshape(1, d))


def kernel(x, c, w_mod, b_mod, w_in, w_pool, ls_pool, conv_w, conv_b, ln_conv_g, ln_conv_b,
           w_sgu, b_sgu, ln_sgu_g, ln_sgu_b, w_out, ln_mix_g, ln_mix_b, w_ff1, w_ff2, ln_ff_g, ln_ff_b):
    batch = x.shape[0]
    mod = _mod_call(c, w_mod, b_mod).reshape(DEPTH, batch, N_MOD, D_MODEL)
    for l in range(DEPTH):
        b_sgu_rows = jnp.repeat(b_sgu[l].T, SGU_HEAD_DIM, axis=1)
        x = _mixer_call(x, mod[l], w_in[l].astype(BF16), w_pool[l].astype(BF16), ls_pool[l],
                        conv_w[l], conv_b[l], ln_conv_g[l], ln_conv_b[l],
                        w_sgu[l], b_sgu_rows, ln_sgu_g[l], ln_sgu_b[l],
                        w_out[l].astype(BF16), ln_mix_g[l], ln_mix_b[l])
        x = _ffn_call(x, mod[l], w_ff1[l].astype(BF16), w_ff2[l].astype(BF16), ln_ff_g[l], ln_ff_b[l])
    return x
```

```python
import jax
import jax.numpy as jnp
from jax import lax
from jax.experimental import pallas as pl
from jax.experimental.pallas import tpu as pltpu

D_MODEL = 2048
DEPTH = 2
WIDTH_POOL = D_MODEL // 4
WIDTH_CONV = (3 * D_MODEL) // 8
WIDTH_SGU = D_MODEL - WIDTH_POOL - WIDTH_CONV
POOL_WINDOWS = (2, 4, 8, 16)
POOL_GROUP_DIM = WIDTH_POOL // len(POOL_WINDOWS)
CONV_WIDTH = 31
N_SGU_HEADS = 6
SGU_HEAD_DIM = WIDTH_SGU // N_SGU_HEADS
CHUNK = 128
D_FF = 4 * D_MODEL
IN_WIDTH = WIDTH_POOL + 2 * WIDTH_CONV + 2 * WIDTH_SGU
N_MOD = 6
LN_EPS = 1e-5
ALPHA = (2.0 * DEPTH) ** 0.25

OFF_VAL = WIDTH_POOL
OFF_GATE = OFF_VAL + WIDTH_CONV
OFF_Z = OFF_GATE + WIDTH_CONV

V7X_VMEM_BYTES = 64 * 1024 * 1024
VMEM_LIMIT_BYTES = V7X_VMEM_BYTES - 8 * 1024 * 1024
LANES = 128
SUBLANES = 8

ROW_STRIDE = 4
ROW_GROUP = ROW_STRIDE * SUBLANES

HALO = 32
MIX_ROWS = 256
LN_ROWS = 16
FFN_ROWS = 1024
FFN_COLS = 512
MOD_COLS = 1024

N_POOL_SLABS = WIDTH_POOL // LANES
N_CONV_SLABS = WIDTH_CONV // LANES

F32 = jnp.float32
BF16 = jnp.bfloat16


def _layer_norm(x, g, b):
    mu = jnp.mean(x, axis=-1, keepdims=True)
    xc = x - mu
    var = jnp.mean(xc * xc, axis=-1, keepdims=True)
    return xc * lax.rsqrt(var + LN_EPS) * g + b


def _strided_rows(start):
    return pl.ds(start, SUBLANES, stride=ROW_STRIDE)


def _mod_kernel(c_ref, w_ref, b_ref, o_ref):
    c = c_ref[...]
    c_act = (c * jax.nn.sigmoid(c)).astype(BF16)
    w = w_ref[...].astype(BF16)
    o_ref[...] = jnp.dot(c_act, w, preferred_element_type=F32) + b_ref[...]


def _mod_call(c, w_mod, b_mod):
    depth, d, n = w_mod.shape
    batch = c.shape[0]
    return pl.pallas_call(
        _mod_kernel,
        grid=(depth, n // MOD_COLS),
        in_specs=[
            pl.BlockSpec((batch, d), lambda l, j: (0, 0)),
            pl.BlockSpec((None, d, MOD_COLS), lambda l, j: (l, 0, j)),
            pl.BlockSpec((None, 1, MOD_COLS), lambda l, j: (l, 0, j)),
        ],
        out_specs=pl.BlockSpec((None, batch, MOD_COLS), lambda l, j: (l, 0, j)),
        out_shape=jax.ShapeDtypeStruct((depth, batch, n), F32),
        compiler_params=pltpu.CompilerParams(
            dimension_semantics=("arbitrary", "arbitrary"), vmem_limit_bytes=VMEM_LIMIT_BYTES),
        name="mod",
    )(c, w_mod, b_mod.reshape(depth, 1, n))


def _mixer_kernel(x_ref, mod_ref, w_in_ref, w_pool_ref, ls_pool_ref, conv_w_ref, conv_b_ref,
                  ln_conv_g_ref, ln_conv_b_ref, w_sgu_ref, b_sgu_ref, ln_sgu_g_ref, ln_sgu_b_ref,
                  w_out_ref, ln_g_ref, ln_b_ref, o_ref, a_hist, g_hist, pool_stage, conv_stage, ycat):
    ts = x_ref.shape[0]
    j = pl.program_id(1)

    @pl.when(j == 0)
    def _():
        a_hist[:, 0:HALO, :] = jnp.zeros((N_POOL_SLABS, HALO, LANES), F32)
        g_hist[:, 0:HALO, :] = jnp.zeros((N_CONV_SLABS, HALO, LANES), F32)

    shift = mod_ref[0:1, :]
    scale = mod_ref[1:2, :]
    gate = mod_ref[2:3, :]
    h = (x_ref[...] * (1.0 + scale) + shift).astype(BF16)

    a = jnp.dot(h, w_in_ref[:, 0:WIDTH_POOL], preferred_element_type=F32)
    for s in range(N_POOL_SLABS):
        a_hist[s, HALO:HALO + ts, :] = a[:, s * LANES:(s + 1) * LANES]
    row_in_group = lax.broadcasted_iota(jnp.int32, (SUBLANES, LANES), 0) * ROW_STRIDE
    for gi, win in enumerate(POOL_WINDOWS):
        for r0 in range(0, ts, ROW_GROUP):
            for q in range(ROW_STRIDE):
                cur = a_hist[gi, _strided_rows(HALO + r0 + q), :]
                wsum = cur
                for back in range(1, win):
                    wsum = wsum + a_hist[gi, _strided_rows(HALO + r0 + q - back), :]
                pos = row_in_group + (j * ts + (r0 + q + 1))
                count = jnp.minimum(pos, win).astype(F32)
                pool_stage[gi, _strided_rows(r0 + q), :] = wsum / count - cur
    a_hist[:, 0:HALO, :] = a_hist[:, ts:ts + HALO, :]
    for gi in range(N_POOL_SLABS):
        lanes = slice(gi * LANES, (gi + 1) * LANES)
        y = jnp.dot(pool_stage[gi].astype(BF16), w_pool_ref[gi], preferred_element_type=F32)
        ycat[:, lanes] = (y * ls_pool_ref[:, lanes]).astype(BF16)

    val = jnp.dot(h, w_in_ref[:, OFF_VAL:OFF_GATE], preferred_element_type=F32)
    gt = jnp.dot(h, w_in_ref[:, OFF_GATE:OFF_Z], preferred_element_type=F32)
    glu = val * jax.nn.sigmoid(gt)
    for s in range(N_CONV_SLABS):
        g_hist[s, HALO:HALO + ts, :] = glu[:, s * LANES:(s + 1) * LANES]
    for r0 in range(0, ts, ROW_GROUP):
        acc = [[jnp.broadcast_to(conv_b_ref[s], (SUBLANES, LANES)) for _ in range(ROW_STRIDE)]
               for s in range(N_CONV_SLABS)]
        for k in range(CONV_WIDTH):
            back = CONV_WIDTH - 1 - k
            for s in range(N_CONV_SLABS):
                w_k = conv_w_ref[s, k:k + 1, :]
                for q in range(ROW_STRIDE):
                    acc[s][q] = acc[s][q] + w_k * g_hist[s, _strided_rows(HALO + r0 + q - back), :]
        for q in range(ROW_STRIDE):
            tot = acc[0][q]
            for s in range(1, N_CONV_SLABS):
                tot = tot + acc[s][q]
            mu = jnp.sum(tot, axis=-1, keepdims=True) / WIDTH_CONV
            xc = [acc[s][q] - mu for s in range(N_CONV_SLABS)]
            sq = xc[0] * xc[0]
            for s in range(1, N_CONV_SLABS):
                sq = sq + xc[s] * xc[s]
            rstd = lax.rsqrt(jnp.sum(sq, axis=-1, keepdims=True) / WIDTH_CONV + LN_EPS)
            for s in range(N_CONV_SLABS):
                hn = xc[s] * rstd * ln_conv_g_ref[s] + ln_conv_b_ref[s]
                conv_stage[s, _strided_rows(r0 + q), :] = hn * jax.nn.sigmoid(hn)
    g_hist[:, 0:HALO, :] = g_hist[:, ts:ts + HALO, :]
    for s in range(N_CONV_SLABS):
        ycat[:, OFF_VAL + s * LANES:OFF_VAL + (s + 1) * LANES] = conv_stage[s].astype(BF16)

    z = jnp.dot(h, w_in_ref[:, OFF_Z:IN_WIDTH], preferred_element_type=F32)
    z = 0.5 * z * (1.0 + lax.erf(z * (2.0 ** -0.5)))
    u = z[:, 0:WIDTH_SGU]
    v = _layer_norm(z[:, WIDTH_SGU:], ln_sgu_g_ref[...], ln_sgu_b_ref[...]).astype(BF16)
    n_chunks = ts // CHUNK
    row = lax.broadcasted_iota(jnp.int32, (CHUNK, CHUNK), 0)
    col = lax.broadcasted_iota(jnp.int32, (CHUNK, CHUNK), 1)
    causal = col <= row
    for hd in range(N_SGU_HEADS):
        lanes = slice(hd * SGU_HEAD_DIM, (hd + 1) * SGU_HEAD_DIM)
        w_h = jnp.where(causal, w_sgu_ref[hd], 0.0).astype(BF16)
        v_h = jnp.concatenate([v[n * CHUNK:(n + 1) * CHUNK, lanes] for n in range(n_chunks)], axis=1)
        m_h = jnp.dot(w_h, v_h, preferred_element_type=F32)
        for n in range(n_chunks):
            rows = slice(n * CHUNK, (n + 1) * CHUNK)
            mixed = m_h[:, n * SGU_HEAD_DIM:(n + 1) * SGU_HEAD_DIM] + b_sgu_ref[:, lanes]
            ycat[rows, OFF_GATE + hd * SGU_HEAD_DIM:OFF_GATE + (hd + 1) * SGU_HEAD_DIM] = (
                u[rows, lanes] * mixed).astype(BF16)

    y = jnp.dot(ycat[...], w_out_ref[...], preferred_element_type=F32)
    o_ref[...] = ALPHA * x_ref[...] + gate * y
    for r0 in range(0, ts, LN_ROWS):
        rows = slice(r0, r0 + LN_ROWS)
        o_ref[rows, :] = _layer_norm(o_ref[rows, :], ln_g_ref[...], ln_b_ref[...])


def _mixer_call(x, mod, w_in, w_pool, ls_pool, conv_w, conv_b, ln_conv_g, ln_conv_b,
                w_sgu, b_sgu_rows, ln_sgu_g, ln_sgu_b, w_out, ln_g, ln_b):
    batch, seq, d = x.shape
    ts = MIX_ROWS
    const2 = lambda b, j: (0, 0)
    const3 = lambda b, j: (0, 0, 0)
    resident = pl.Buffered(1)
    row = lambda v: v.reshape(1, -1)
    slab_rows = lambda v: v.reshape(N_CONV_SLABS, 1, LANES)
    conv_w_slabs = conv_w.reshape(CONV_WIDTH, N_CONV_SLABS, LANES).transpose(1, 0, 2)
    return pl.pallas_call(
        _mixer_kernel,
        grid=(batch, seq // ts),
        in_specs=[
            pl.BlockSpec((None, ts, d), lambda b, j: (b, j, 0)),
            pl.BlockSpec((None, N_MOD, d), lambda b, j: (b, 0, 0)),
            pl.BlockSpec(w_in.shape, const2, pipeline_mode=resident),
            pl.BlockSpec(w_pool.shape, const3),
            pl.BlockSpec((1, WIDTH_POOL), const2),
            pl.BlockSpec((N_CONV_SLABS, CONV_WIDTH, LANES), const3),
            pl.BlockSpec((N_CONV_SLABS, 1, LANES), const3),
            pl.BlockSpec((N_CONV_SLABS, 1, LANES), const3),
            pl.BlockSpec((N_CONV_SLABS, 1, LANES), const3),
            pl.BlockSpec(w_sgu.shape, const3),
            pl.BlockSpec(b_sgu_rows.shape, const2),
            pl.BlockSpec((1, WIDTH_SGU), const2),
            pl.BlockSpec((1, WIDTH_SGU), const2),
            pl.BlockSpec(w_out.shape, const2, pipeline_mode=resident),
            pl.BlockSpec((1, d), const2),
            pl.BlockSpec((1, d), const2),
        ],
        out_specs=pl.BlockSpec((None, ts, d), lambda b, j: (b, j, 0)),
        out_shape=jax.ShapeDtypeStruct(x.shape, F32),
        scratch_shapes=[
            pltpu.VMEM((N_POOL_SLABS, HALO + ts, LANES), F32),
            pltpu.VMEM((N_CONV_SLABS, HALO + ts, LANES), F32),
            pltpu.VMEM((N_POOL_SLABS, ts, LANES), F32),
            pltpu.VMEM((N_CONV_SLABS, ts, LANES), F32),
            pltpu.VMEM((ts, d), BF16),
        ],
        compiler_params=pltpu.CompilerParams(
            dimension_semantics=("arbitrary", "arbitrary"), vmem_limit_bytes=VMEM_LIMIT_BYTES),
        name="mixer",
    )(x, mod, w_in, w_pool, row(ls_pool), conv_w_slabs, slab_rows(conv_b), slab_rows(ln_conv_g),
      slab_rows(ln_conv_b), w_sgu, b_sgu_rows, row(ln_sgu_g), row(ln_sgu_b), w_out, row(ln_g), row(ln_b))


def _ffn_kernel(x_ref, mod_ref, w1_ref, w2_ref, ln_g_ref, ln_b_ref, o_ref, h_ref):
    k = pl.program_id(1)

    @pl.when(k == 0)
    def _():
        x = x_ref[...]
        h_ref[...] = (x * (1.0 + mod_ref[4:5, :]) + mod_ref[3:4, :]).astype(BF16)
        o_ref[...] = ALPHA * x

    a = jnp.dot(h_ref[...], w1_ref[...], preferred_element_type=F32)
    a = jnp.maximum(a, 0.0)
    part = jnp.dot((a * a).astype(BF16), w2_ref[...], preferred_element_type=F32)
    o_ref[...] += mod_ref[5:6, :] * part

    @pl.when(k == pl.num_programs(1) - 1)
    def _():
        o_ref[...] = _layer_norm(o_ref[...], ln_g_ref[...], ln_b_ref[...])


def _ffn_call(x, mod, w1, w2, ln_g, ln_b):
    batch, seq, d = x.shape
    d_ff = w1.shape[1]
    tm = FFN_ROWS
    tiles_per_seq = seq // tm
    x_map = lambda i, k: (i // tiles_per_seq, i % tiles_per_seq, 0)
    return pl.pallas_call(
        _ffn_kernel,
        grid=(batch * tiles_per_seq, d_ff // FFN_COLS),
        in_specs=[
            pl.BlockSpec((None, tm, d), x_map, pipeline_mode=pl.Buffered(1)),
            pl.BlockSpec((None, N_MOD, d), lambda i, k: (i // tiles_per_seq, 0, 0)),
            pl.BlockSpec((d, FFN_COLS), lambda i, k: (0, k)),
            pl.BlockSpec((FFN_COLS, d), lambda i, k: (k, 0)),
            pl.BlockSpec((1, d), lambda i, k: (0, 0)),
            pl.BlockSpec((1, d), lambda i, k: (0, 0)),
        ],
        out_specs=pl.BlockSpec((None, tm, d), x_map),
        out_shape=jax.ShapeDtypeStruct(x.shape, F32),
        scratch_shapes=[pltpu.VMEM((tm, d), BF16)],
        compiler_params=pltpu.CompilerParams(
            dimension_semantics=("arbitrary", "arbitrary"), vmem_limit_bytes=VMEM_LIMIT_BYTES),
        name="ffn",
    )(x, mod, w1, w2, ln_g.reshape(1, d), ln_b.reshape(1, d))


def kernel(x, c, w_mod, b_mod, w_in, w_pool, ls_pool, conv_w, conv_b, ln_conv_g, ln_conv_b,
           w_sgu, b_sgu, ln_sgu_g, ln_sgu_b, w_out, ln_mix_g, ln_mix_b, w_ff1, w_ff2, ln_ff_g, ln_ff_b):
    batch = x.shape[0]
    mod = _mod_call(c, w_mod, b_mod).reshape(DEPTH, batch, N_MOD, D_MODEL)
    for l in range(DEPTH):
        b_sgu_rows = jnp.repeat(b_sgu[l].T, SGU_HEAD_DIM, axis=1)
        x = _mixer_call(x, mod[l], w_in[l].astype(BF16), w_pool[l].astype(BF16), ls_pool[l],
                        conv_w[l], conv_b[l], ln_conv_g[l], ln_conv_b[l],
                        w_sgu[l], b_sgu_rows, ln_sgu_g[l], ln_sgu_b[l],
                        w_out[l].astype(BF16), ln_mix_g[l], ln_mix_b[l])
        x = _ffn_call(x, mod[l], w_ff1[l].astype(BF16), w_ff2[l].astype(BF16), ln_ff_g[l], ln_ff_b[l])
    return x
```

```python
import jax
import jax.numpy as jnp
from jax import lax
from jax.experimental import pallas as pl
from jax.experimental.pallas import tpu as pltpu

D_MODEL = 2048
DEPTH = 2
WIDTH_POOL = D_MODEL // 4
WIDTH_CONV = (3 * D_MODEL) // 8
WIDTH_SGU = D_MODEL - WIDTH_POOL - WIDTH_CONV
POOL_WINDOWS = (2, 4, 8, 16)
POOL_GROUP_DIM = WIDTH_POOL // len(POOL_WINDOWS)
CONV_WIDTH = 31
N_SGU_HEADS = 6
SGU_HEAD_DIM = WIDTH_SGU // N_SGU_HEADS
CHUNK = 128
D_FF = 4 * D_MODEL
IN_WIDTH = WIDTH_POOL + 2 * WIDTH_CONV + 2 * WIDTH_SGU
N_MOD = 6
LN_EPS = 1e-5
ALPHA = (2.0 * DEPTH) ** 0.25

OFF_VAL = WIDTH_POOL
OFF_GATE = OFF_VAL + WIDTH_CONV
OFF_Z = OFF_GATE + WIDTH_CONV

V7X_VMEM_BYTES = 64 * 1024 * 1024
VMEM_LIMIT_BYTES = V7X_VMEM_BYTES - 8 * 1024 * 1024
FFN_VMEM_LIMIT_BYTES = V7X_VMEM_BYTES - 2 * 1024 * 1024
LANES = 128
SUBLANES = 8

ROW_STRIDE = 4
ROW_GROUP = ROW_STRIDE * SUBLANES

HALO = 32
MIX_ROWS = 256
LN_ROWS = 16
FFN_ROWS = 1024
FFN_COLS = 512
FFN_TAIL_ROWS = 256
MOD_COLS = 1024

N_POOL_SLABS = WIDTH_POOL // LANES
N_CONV_SLABS = WIDTH_CONV // LANES

F32 = jnp.float32
BF16 = jnp.bfloat16


def _layer_norm(x, g, b):
    mu = jnp.mean(x, axis=-1, keepdims=True)
    xc = x - mu
    var = jnp.mean(xc * xc, axis=-1, keepdims=True)
    return xc * lax.rsqrt(var + LN_EPS) * g + b


def _strided_rows(start):
    return pl.ds(start, SUBLANES, stride=ROW_STRIDE)


def _mod_kernel(c_ref, w_ref, b_ref, o_ref):
    c = c_ref[...]
    c_act = (c * jax.nn.sigmoid(c)).astype(BF16)
    w = w_ref[...].astype(BF16)
    o_ref[...] = jnp.dot(c_act, w, preferred_element_type=F32) + b_ref[...]


def _mod_call(c, w_mod, b_mod):
    depth, d, n = w_mod.shape
    batch = c.shape[0]
    return pl.pallas_call(
        _mod_kernel,
        grid=(depth, n // MOD_COLS),
        in_specs=[
            pl.BlockSpec((batch, d), lambda l, j: (0, 0)),
            pl.BlockSpec((None, d, MOD_COLS), lambda l, j: (l, 0, j)),
            pl.BlockSpec((None, 1, MOD_COLS), lambda l, j: (l, 0, j)),
        ],
        out_specs=pl.BlockSpec((None, batch, MOD_COLS), lambda l, j: (l, 0, j)),
        out_shape=jax.ShapeDtypeStruct((depth, batch, n), F32),
        compiler_params=pltpu.CompilerParams(
            dimension_semantics=("arbitrary", "arbitrary"), vmem_limit_bytes=VMEM_LIMIT_BYTES),
        name="mod",
    )(c, w_mod, b_mod.reshape(depth, 1, n))


def _mixer_kernel(x_ref, mod_ref, w_in_ref, w_pool_ref, ls_pool_ref, conv_w_ref, conv_b_ref,
                  ln_conv_g_ref, ln_conv_b_ref, w_sgu_ref, b_sgu_ref, ln_sgu_g_ref, ln_sgu_b_ref,
                  w_out_ref, ln_g_ref, ln_b_ref, o_ref, a_hist, g_hist, pool_stage, conv_stage, ycat):
    ts = x_ref.shape[0]
    j = pl.program_id(1)

    @pl.when(j == 0)
    def _():
        a_hist[:, 0:HALO, :] = jnp.zeros((N_POOL_SLABS, HALO, LANES), F32)
        g_hist[:, 0:HALO, :] = jnp.zeros((N_CONV_SLABS, HALO, LANES), F32)

    shift = mod_ref[0:1, :]
    scale = mod_ref[1:2, :]
    gate = mod_ref[2:3, :]
    h = (x_ref[...] * (1.0 + scale) + shift).astype(BF16)

    a = jnp.dot(h, w_in_ref[:, 0:WIDTH_POOL], preferred_element_type=F32)
    for s in range(N_POOL_SLABS):
        a_hist[s, HALO:HALO + ts, :] = a[:, s * LANES:(s + 1) * LANES]
    row_in_group = lax.broadcasted_iota(jnp.int32, (SUBLANES, LANES), 0) * ROW_STRIDE
    for gi, win in enumerate(POOL_WINDOWS):
        for r0 in range(0, ts, ROW_GROUP):
            for q in range(ROW_STRIDE):
                cur = a_hist[gi, _strided_rows(HALO + r0 + q), :]
                wsum = cur
                for back in range(1, win):
                    wsum = wsum + a_hist[gi, _strided_rows(HALO + r0 + q - back), :]
                pos = row_in_group + (j * ts + (r0 + q + 1))
                count = jnp.minimum(pos, win).astype(F32)
                pool_stage[gi, _strided_rows(r0 + q), :] = wsum / count - cur
    a_hist[:, 0:HALO, :] = a_hist[:, ts:ts + HALO, :]
    for gi in range(N_POOL_SLABS):
        lanes = slice(gi * LANES, (gi + 1) * LANES)
        y = jnp.dot(pool_stage[gi].astype(BF16), w_pool_ref[gi], preferred_element_type=F32)
        ycat[:, lanes] = (y * ls_pool_ref[:, lanes]).astype(BF16)

    val = jnp.dot(h, w_in_ref[:, OFF_VAL:OFF_GATE], preferred_element_type=F32)
    gt = jnp.dot(h, w_in_ref[:, OFF_GATE:OFF_Z], preferred_element_type=F32)
    glu = val * jax.nn.sigmoid(gt)
    for s in range(N_CONV_SLABS):
        g_hist[s, HALO:HALO + ts, :] = glu[:, s * LANES:(s + 1) * LANES]
    for r0 in range(0, ts, ROW_GROUP):
        acc = [[jnp.broadcast_to(conv_b_ref[s], (SUBLANES, LANES)) for _ in range(ROW_STRIDE)]
               for s in range(N_CONV_SLABS)]
        for k in range(CONV_WIDTH):
            back = CONV_WIDTH - 1 - k
            for s in range(N_CONV_SLABS):
                w_k = conv_w_ref[s, k:k + 1, :]
                for q in range(ROW_STRIDE):
                    acc[s][q] = acc[s][q] + w_k * g_hist[s, _strided_rows(HALO + r0 + q - back), :]
        for q in range(ROW_STRIDE):
            tot = acc[0][q]
            for s in range(1, N_CONV_SLABS):
                tot = tot + acc[s][q]
            mu = jnp.sum(tot, axis=-1, keepdims=True) / WIDTH_CONV
            xc = [acc[s][q] - mu for s in range(N_CONV_SLABS)]
            sq = xc[0] * xc[0]
            for s in range(1, N_CONV_SLABS):
                sq = sq + xc[s] * xc[s]
            rstd = lax.rsqrt(jnp.sum(sq, axis=-1, keepdims=True) / WIDTH_CONV + LN_EPS)
            for s in range(N_CONV_SLABS):
                hn = xc[s] * rstd * ln_conv_g_ref[s] + ln_conv_b_ref[s]
                conv_stage[s, _strided_rows(r0 + q), :] = hn * jax.nn.sigmoid(hn)
    g_hist[:, 0:HALO, :] = g_hist[:, ts:ts + HALO, :]
    for s in range(N_CONV_SLABS):
        ycat[:, OFF_VAL + s * LANES:OFF_VAL + (s + 1) * LANES] = conv_stage[s].astype(BF16)

    z = jnp.dot(h, w_in_ref[:, OFF_Z:IN_WIDTH], preferred_element_type=F32)
    z = 0.5 * z * (1.0 + lax.erf(z * (2.0 ** -0.5)))
    u = z[:, 0:WIDTH_SGU]
    v = _layer_norm(z[:, WIDTH_SGU:], ln_sgu_g_ref[...], ln_sgu_b_ref[...]).astype(BF16)
    n_chunks = ts // CHUNK
    row = lax.broadcasted_iota(jnp.int32, (CHUNK, CHUNK), 0)
    col = lax.broadcasted_iota(jnp.int32, (CHUNK, CHUNK), 1)
    causal = col <= row
    for hd in range(N_SGU_HEADS):
        lanes = slice(hd * SGU_HEAD_DIM, (hd + 1) * SGU_HEAD_DIM)
        w_h = jnp.where(causal, w_sgu_ref[hd], 0.0).astype(BF16)
        v_h = jnp.concatenate([v[n * CHUNK:(n + 1) * CHUNK, lanes] for n in range(n_chunks)], axis=1)
        m_h = jnp.dot(w_h, v_h, preferred_element_type=F32)
        for n in range(n_chunks):
            rows = slice(n * CHUNK, (n + 1) * CHUNK)
            mixed = m_h[:, n * SGU_HEAD_DIM:(n + 1) * SGU_HEAD_DIM] + b_sgu_ref[:, lanes]
            ycat[rows, OFF_GATE + hd * SGU_HEAD_DIM:OFF_GATE + (hd + 1) * SGU_HEAD_DIM] = (
                u[rows, lanes] * mixed).astype(BF16)

    y = jnp.dot(ycat[...], w_out_ref[...], preferred_element_type=F32)
    o_ref[...] = ALPHA * x_ref[...] + gate * y
    for r0 in range(0, ts, LN_ROWS):
        rows = slice(r0, r0 + LN_ROWS)
        o_ref[rows, :] = _layer_norm(o_ref[rows, :], ln_g_ref[...], ln_b_ref[...])


def _mixer_call(x, mod, w_in, w_pool, ls_pool, conv_w, conv_b, ln_conv_g, ln_conv_b,
                w_sgu, b_sgu_rows, ln_sgu_g, ln_sgu_b, w_out, ln_g, ln_b):
    batch, seq, d = x.shape
    ts = MIX_ROWS
    const2 = lambda b, j: (0, 0)
    const3 = lambda b, j: (0, 0, 0)
    resident = pl.Buffered(1)
    row = lambda v: v.reshape(1, -1)
    slab_rows = lambda v: v.reshape(N_CONV_SLABS, 1, LANES)
    conv_w_slabs = conv_w.reshape(CONV_WIDTH, N_CONV_SLABS, LANES).transpose(1, 0, 2)
    return pl.pallas_call(
        _mixer_kernel,
        grid=(batch, seq // ts),
        in_specs=[
            pl.BlockSpec((None, ts, d), lambda b, j: (b, j, 0)),
            pl.BlockSpec((None, N_MOD, d), lambda b, j: (b, 0, 0)),
            pl.BlockSpec(w_in.shape, const2, pipeline_mode=resident),
            pl.BlockSpec(w_pool.shape, const3),
            pl.BlockSpec((1, WIDTH_POOL), const2),
            pl.BlockSpec((N_CONV_SLABS, CONV_WIDTH, LANES), const3),
            pl.BlockSpec((N_CONV_SLABS, 1, LANES), const3),
            pl.BlockSpec((N_CONV_SLABS, 1, LANES), const3),
            pl.BlockSpec((N_CONV_SLABS, 1, LANES), const3),
            pl.BlockSpec(w_sgu.shape, const3),
            pl.BlockSpec(b_sgu_rows.shape, const2),
            pl.BlockSpec((1, WIDTH_SGU), const2),
            pl.BlockSpec((1, WIDTH_SGU), const2),
            pl.BlockSpec(w_out.shape, const2, pipeline_mode=resident),
            pl.BlockSpec((1, d), const2),
            pl.BlockSpec((1, d), const2),
        ],
        out_specs=pl.BlockSpec((None, ts, d), lambda b, j: (b, j, 0)),
        out_shape=jax.ShapeDtypeStruct(x.shape, F32),
        scratch_shapes=[
            pltpu.VMEM((N_POOL_SLABS, HALO + ts, LANES), F32),
            pltpu.VMEM((N_CONV_SLABS, HALO + ts, LANES), F32),
            pltpu.VMEM((N_POOL_SLABS, ts, LANES), F32),
            pltpu.VMEM((N_CONV_SLABS, ts, LANES), F32),
            pltpu.VMEM((ts, d), BF16),
        ],
        compiler_params=pltpu.CompilerParams(
            dimension_semantics=("arbitrary", "arbitrary"), vmem_limit_bytes=VMEM_LIMIT_BYTES),
        name="mixer",
    )(x, mod, w_in, w_pool, row(ls_pool), conv_w_slabs, slab_rows(conv_b), slab_rows(ln_conv_g),
      slab_rows(ln_conv_b), w_sgu, b_sgu_rows, row(ln_sgu_g), row(ln_sgu_b), w_out, row(ln_g), row(ln_b))


def _ffn_kernel(x_ref, mod_ref, w1_ref, w2_ref, ln_g_ref, ln_b_ref, o_ref, h_ref):
    k = pl.program_id(1)
    last = pl.num_programs(1) - 1
    gate = mod_ref[5:6, :]

    def hidden():
        a = jnp.maximum(jnp.dot(h_ref[...], w1_ref[...], preferred_element_type=F32), 0.0)
        return (a * a).astype(BF16)

    @pl.when(k == 0)
    def _():
        h_ref[...] = (x_ref[...] * (1.0 + mod_ref[4:5, :]) + mod_ref[3:4, :]).astype(BF16)
        o_ref[...] = ALPHA * x_ref[...] + gate * jnp.dot(hidden(), w2_ref[...], preferred_element_type=F32)

    @pl.when(jnp.logical_and(k > 0, k < last))
    def _():
        o_ref[...] += gate * jnp.dot(hidden(), w2_ref[...], preferred_element_type=F32)

    @pl.when(k == last)
    def _():
        a2 = hidden()
        blocks = [slice(r0, r0 + FFN_TAIL_ROWS) for r0 in range(0, o_ref.shape[0], FFN_TAIL_ROWS)]

        def project(rows):
            o_ref[rows, :] += gate * jnp.dot(a2[rows, :], w2_ref[...], preferred_element_type=F32)

        project(blocks[0])
        for done, rows in zip(blocks, blocks[1:]):
            project(rows)
            o_ref[done, :] = _layer_norm(o_ref[done, :], ln_g_ref[...], ln_b_ref[...])
        o_ref[blocks[-1], :] = _layer_norm(o_ref[blocks[-1], :], ln_g_ref[...], ln_b_ref[...])


def _ffn_call(x, mod, w1_chunks, w2, ln_g, ln_b):
    batch, seq, d = x.shape
    n_chunks = w1_chunks.shape[0]
    tm = FFN_ROWS
    tiles_per_seq = seq // tm
    x_map = lambda i, k: (i // tiles_per_seq, i % tiles_per_seq, 0)
    return pl.pallas_call(
        _ffn_kernel,
        grid=(batch * tiles_per_seq, n_chunks),
        in_specs=[
            pl.BlockSpec((None, tm, d), x_map),
            pl.BlockSpec((None, N_MOD, d), lambda i, k: (i // tiles_per_seq, 0, 0)),
            pl.BlockSpec((None, d, FFN_COLS), lambda i, k: (k, 0, 0)),
            pl.BlockSpec((FFN_COLS, d), lambda i, k: (k, 0)),
            pl.BlockSpec((1, d), lambda i, k: (0, 0)),
            pl.BlockSpec((1, d), lambda i, k: (0, 0)),
        ],
        out_specs=pl.BlockSpec((None, tm, d), x_map),
        out_shape=jax.ShapeDtypeStruct(x.shape, F32),
        scratch_shapes=[pltpu.VMEM((tm, d), BF16)],
        compiler_params=pltpu.CompilerParams(
            dimension_semantics=("arbitrary", "arbitrary"), vmem_limit_bytes=FFN_VMEM_LIMIT_BYTES),
        name="ffn",
    )(x, mod, w1_chunks, w2, ln_g.reshape(1, d), ln_b.reshape(1, d))


def kernel(x, c, w_mod, b_mod, w_in, w_pool, ls_pool, conv_w, conv_b, ln_conv_g, ln_conv_b,
           w_sgu, b_sgu, ln_sgu_g, ln_sgu_b, w_out, ln_mix_g, ln_mix_b, w_ff1, w_ff2, ln_ff_g, ln_ff_b):
    batch = x.shape[0]
    mod = _mod_call(c, w_mod, b_mod).reshape(DEPTH, batch, N_MOD, D_MODEL)
    for l in range(DEPTH):
        b_sgu_rows = jnp.repeat(b_sgu[l].T, SGU_HEAD_DIM, axis=1)
        x = _mixer_call(x, mod[l], w_in[l].astype(BF16), w_pool[l].astype(BF16), ls_pool[l],
                        conv_w[l], conv_b[l], ln_conv_g[l], ln_conv_b[l],
                        w_sgu[l], b_sgu_rows, ln_sgu_g[l], ln_sgu_b[l],
                        w_out[l].astype(BF16), ln_mix_g[l], ln_mix_b[l])
        w1_chunks = w_ff1[l].astype(BF16).reshape(D_MODEL, D_FF // FFN_COLS, FFN_COLS).transpose(1, 0, 2)
        x = _ffn_call(x, mod[l], w1_chunks, w_ff2[l].astype(BF16), ln_ff_g[l], ln_ff_b[l])
    return x
```

```python
import jax
import jax.numpy as jnp
from jax import lax
from jax.experimental import pallas as pl
from jax.experimental.pallas import tpu as pltpu

D_MODEL = 2048
DEPTH = 2
WIDTH_POOL = D_MODEL // 4
WIDTH_CONV = (3 * D_MODEL) // 8
WIDTH_SGU = D_MODEL - WIDTH_POOL - WIDTH_CONV
POOL_WINDOWS = (2, 4, 8, 16)
POOL_GROUP_DIM = WIDTH_POOL // len(POOL_WINDOWS)
CONV_WIDTH = 31
N_SGU_HEADS = 6
SGU_HEAD_DIM = WIDTH_SGU // N_SGU_HEADS
CHUNK = 128
D_FF = 4 * D_MODEL
IN_WIDTH = WIDTH_POOL + 2 * WIDTH_CONV + 2 * WIDTH_SGU
N_MOD = 6
LN_EPS = 1e-5
ALPHA = (2.0 * DEPTH) ** 0.25

OFF_VAL = WIDTH_POOL
OFF_GATE = OFF_VAL + WIDTH_CONV
OFF_Z = OFF_GATE + WIDTH_CONV

V7X_VMEM_BYTES = 64 * 1024 * 1024
VMEM_LIMIT_BYTES = V7X_VMEM_BYTES - 8 * 1024 * 1024
FFN_VMEM_LIMIT_BYTES = V7X_VMEM_BYTES - 2 * 1024 * 1024
LANES = 128
SUBLANES = 8

ROW_STRIDE = 4
ROW_GROUP = ROW_STRIDE * SUBLANES

HALO = 32
MIX_ROWS = 256
LN_ROWS = 16
FFN_ROWS = 1024
FFN_COLS = 512
FFN_TAIL_ROWS = 256
MOD_COLS = 1024

N_POOL_SLABS = WIDTH_POOL // LANES
N_CONV_SLABS = WIDTH_CONV // LANES

F32 = jnp.float32
BF16 = jnp.bfloat16


def _layer_norm(x, g, b):
    mu = jnp.mean(x, axis=-1, keepdims=True)
    xc = x - mu
    var = jnp.mean(xc * xc, axis=-1, keepdims=True)
    return xc * lax.rsqrt(var + LN_EPS) * g + b


def _strided_rows(start):
    return pl.ds(start, SUBLANES, stride=ROW_STRIDE)


def _mod_kernel(c_ref, w_ref, b_ref, o_ref):
    c = c_ref[...]
    c_act = (c * jax.nn.sigmoid(c)).astype(BF16)
    w = w_ref[...].astype(BF16)
    o_ref[...] = jnp.dot(c_act, w, preferred_element_type=F32) + b_ref[...]


def _mod_call(c, w_mod, b_mod):
    depth, d, n = w_mod.shape
    batch = c.shape[0]
    return pl.pallas_call(
        _mod_kernel,
        grid=(depth, n // MOD_COLS),
        in_specs=[
            pl.BlockSpec((batch, d), lambda l, j: (0, 0)),
            pl.BlockSpec((None, d, MOD_COLS), lambda l, j: (l, 0, j)),
            pl.BlockSpec((None, 1, MOD_COLS), lambda l, j: (l, 0, j)),
        ],
        out_specs=pl.BlockSpec((None, batch, MOD_COLS), lambda l, j: (l, 0, j)),
        out_shape=jax.ShapeDtypeStruct((depth, batch, n), F32),
        compiler_params=pltpu.CompilerParams(
            dimension_semantics=("arbitrary", "arbitrary"), vmem_limit_bytes=VMEM_LIMIT_BYTES),
        name="mod",
    )(c, w_mod, b_mod.reshape(depth, 1, n))


def _mixer_kernel(x_ref, mod_ref, w_in_ref, w_pool_ref, ls_pool_ref, conv_w_ref, conv_b_ref,
                  ln_conv_g_ref, ln_conv_b_ref, w_sgu_ref, b_sgu_ref, ln_sgu_g_ref, ln_sgu_b_ref,
                  w_out_ref, ln_g_ref, ln_b_ref, o_ref, a_hist, g_hist, pool_stage, conv_stage, ycat):
    ts = x_ref.shape[0]
    j = pl.program_id(1)

    @pl.when(j == 0)
    def _():
        a_hist[:, 0:HALO, :] = jnp.zeros((N_POOL_SLABS, HALO, LANES), F32)
        g_hist[:, 0:HALO, :] = jnp.zeros((N_CONV_SLABS, HALO, LANES), F32)

    shift = mod_ref[0:1, :]
    scale = mod_ref[1:2, :]
    gate = mod_ref[2:3, :]
    h = (x_ref[...] * (1.0 + scale) + shift).astype(BF16)

    val = jnp.dot(h, w_in_ref[:, OFF_VAL:OFF_GATE], preferred_element_type=F32)
    gt = jnp.dot(h, w_in_ref[:, OFF_GATE:OFF_Z], preferred_element_type=F32)
    glu = val * jax.nn.sigmoid(gt)
    for s in range(N_CONV_SLABS):
        g_hist[s, HALO:HALO + ts, :] = glu[:, s * LANES:(s + 1) * LANES]

    a = jnp.dot(h, w_in_ref[:, 0:WIDTH_POOL], preferred_element_type=F32)
    for s in range(N_POOL_SLABS):
        a_hist[s, HALO:HALO + ts, :] = a[:, s * LANES:(s + 1) * LANES]
    row_in_group = lax.broadcasted_iota(jnp.int32, (SUBLANES, LANES), 0) * ROW_STRIDE
    for gi, win in enumerate(POOL_WINDOWS):
        for r0 in range(0, ts, ROW_GROUP):
            for q in range(ROW_STRIDE):
                cur = a_hist[gi, _strided_rows(HALO + r0 + q), :]
                wsum = cur
                for back in range(1, win):
                    wsum = wsum + a_hist[gi, _strided_rows(HALO + r0 + q - back), :]
                pos = row_in_group + (j * ts + (r0 + q + 1))
                count = jnp.minimum(pos, win).astype(F32)
                pool_stage[gi, _strided_rows(r0 + q), :] = wsum / count - cur
    a_hist[:, 0:HALO, :] = a_hist[:, ts:ts + HALO, :]
    for gi in range(N_POOL_SLABS):
        lanes = slice(gi * LANES, (gi + 1) * LANES)
        y = jnp.dot(pool_stage[gi].astype(BF16), w_pool_ref[gi], preferred_element_type=F32)
        ycat[:, lanes] = (y * ls_pool_ref[:, lanes]).astype(BF16)

    z = jnp.dot(h, w_in_ref[:, OFF_Z:IN_WIDTH], preferred_element_type=F32)
    z = 0.5 * z * (1.0 + lax.erf(z * (2.0 ** -0.5)))
    u = z[:, 0:WIDTH_SGU]
    v = _layer_norm(z[:, WIDTH_SGU:], ln_sgu_g_ref[...], ln_sgu_b_ref[...]).astype(BF16)
    n_chunks = ts // CHUNK
    row = lax.broadcasted_iota(jnp.int32, (CHUNK, CHUNK), 0)
    col = lax.broadcasted_iota(jnp.int32, (CHUNK, CHUNK), 1)
    causal = col <= row
    for hd in range(N_SGU_HEADS):
        lanes = slice(hd * SGU_HEAD_DIM, (hd + 1) * SGU_HEAD_DIM)
        w_h = jnp.where(causal, w_sgu_ref[hd], 0.0).astype(BF16)
        v_h = jnp.concatenate([v[n * CHUNK:(n + 1) * CHUNK, lanes] for n in range(n_chunks)], axis=1)
        m_h = jnp.dot(w_h, v_h, preferred_element_type=F32)
        for n in range(n_chunks):
            rows = slice(n * CHUNK, (n + 1) * CHUNK)
            mixed = m_h[:, n * SGU_HEAD_DIM:(n + 1) * SGU_HEAD_DIM] + b_sgu_ref[:, lanes]
            ycat[rows, OFF_GATE + hd * SGU_HEAD_DIM:OFF_GATE + (hd + 1) * SGU_HEAD_DIM] = (
                u[rows, lanes] * mixed).astype(BF16)

    def out_proj(lo, hi):
        return jnp.dot(ycat[:, lo:hi], w_out_ref[lo:hi, :], preferred_element_type=F32)

    y = out_proj(0, OFF_VAL) + out_proj(OFF_GATE, D_MODEL)
    o_ref[...] = ALPHA * x_ref[...] + gate * y

    for r0 in range(0, ts, ROW_GROUP):
        acc = [[jnp.broadcast_to(conv_b_ref[s], (SUBLANES, LANES)) for _ in range(ROW_STRIDE)]
               for s in range(N_CONV_SLABS)]
        for k in range(CONV_WIDTH):
            back = CONV_WIDTH - 1 - k
            for s in range(N_CONV_SLABS):
                w_k = conv_w_ref[s, k:k + 1, :]
                for q in range(ROW_STRIDE):
                    acc[s][q] = acc[s][q] + w_k * g_hist[s, _strided_rows(HALO + r0 + q - back), :]
        for q in range(ROW_STRIDE):
            tot = acc[0][q]
            for s in range(1, N_CONV_SLABS):
                tot = tot + acc[s][q]
            mu = jnp.sum(tot, axis=-1, keepdims=True) / WIDTH_CONV
            xc = [acc[s][q] - mu for s in range(N_CONV_SLABS)]
            sq = xc[0] * xc[0]
            for s in range(1, N_CONV_SLABS):
                sq = sq + xc[s] * xc[s]
            rstd = lax.rsqrt(jnp.sum(sq, axis=-1, keepdims=True) / WIDTH_CONV + LN_EPS)
            for s in range(N_CONV_SLABS):
                hn = xc[s] * rstd * ln_conv_g_ref[s] + ln_conv_b_ref[s]
                conv_stage[s, _strided_rows(r0 + q), :] = hn * jax.nn.sigmoid(hn)
    g_hist[:, 0:HALO, :] = g_hist[:, ts:ts + HALO, :]
    for s in range(N_CONV_SLABS):
        ycat[:, OFF_VAL + s * LANES:OFF_VAL + (s + 1) * LANES] = conv_stage[s].astype(BF16)

    o_ref[...] += gate * out_proj(OFF_VAL, OFF_GATE)
    for r0 in range(0, ts, LN_ROWS):
        rows = slice(r0, r0 + LN_ROWS)
        o_ref[rows, :] = _layer_norm(o_ref[rows, :], ln_g_ref[...], ln_b_ref[...])


def _mixer_call(x, mod, w_in, w_pool, ls_pool, conv_w, conv_b, ln_conv_g, ln_conv_b,
                w_sgu, b_sgu_rows, ln_sgu_g, ln_sgu_b, w_out, ln_g, ln_b):
    batch, seq, d = x.shape
    ts = MIX_ROWS
    const2 = lambda b, j: (0, 0)
    const3 = lambda b, j: (0, 0, 0)
    resident = pl.Buffered(1)
    row = lambda v: v.reshape(1, -1)
    slab_rows = lambda v: v.reshape(N_CONV_SLABS, 1, LANES)
    conv_w_slabs = conv_w.reshape(CONV_WIDTH, N_CONV_SLABS, LANES).transpose(1, 0, 2)
    return pl.pallas_call(
        _mixer_kernel,
        grid=(batch, seq // ts),
        in_specs=[
            pl.BlockSpec((None, ts, d), lambda b, j: (b, j, 0)),
            pl.BlockSpec((None, N_MOD, d), lambda b, j: (b, 0, 0)),
            pl.BlockSpec(w_in.shape, const2, pipeline_mode=resident),
            pl.BlockSpec(w_pool.shape, const3),
            pl.BlockSpec((1, WIDTH_POOL), const2),
            pl.BlockSpec((N_CONV_SLABS, CONV_WIDTH, LANES), const3),
            pl.BlockSpec((N_CONV_SLABS, 1, LANES), const3),
            pl.BlockSpec((N_CONV_SLABS, 1, LANES), const3),
            pl.BlockSpec((N_CONV_SLABS, 1, LANES), const3),
            pl.BlockSpec(w_sgu.shape, const3),
            pl.BlockSpec(b_sgu_rows.shape, const2),
            pl.BlockSpec((1, WIDTH_SGU), const2),
            pl.BlockSpec((1, WIDTH_SGU), const2),
            pl.BlockSpec(w_out.shape, const2, pipeline_mode=resident),
            pl.BlockSpec((1, d), const2),
            pl.BlockSpec((1, d), const2),
        ],
        out_specs=pl.BlockSpec((None, ts, d), lambda b, j: (b, j, 0)),
        out_shape=jax.ShapeDtypeStruct(x.shape, F32),
        scratch_shapes=[
            pltpu.VMEM((N_POOL_SLABS, HALO + ts, LANES), F32),
            pltpu.VMEM((N_CONV_SLABS, HALO + ts, LANES), F32),
            pltpu.VMEM((N_POOL_SLABS, ts, LANES), F32),
            pltpu.VMEM((N_CONV_SLABS, ts, LANES), F32),
            pltpu.VMEM((ts, d), BF16),
        ],
        compiler_params=pltpu.CompilerParams(
            dimension_semantics=("arbitrary", "arbitrary"), vmem_limit_bytes=VMEM_LIMIT_BYTES),
        name="mixer",
    )(x, mod, w_in, w_pool, row(ls_pool), conv_w_slabs, slab_rows(conv_b), slab_rows(ln_conv_g),
      slab_rows(ln_conv_b), w_sgu, b_sgu_rows, row(ln_sgu_g), row(ln_sgu_b), w_out, row(ln_g), row(ln_b))


def _ffn_kernel(x_ref, mod_ref, w1_ref, w2_ref, ln_g_ref, ln_b_ref, o_ref, h_ref):
    k = pl.program_id(1)
    last = pl.num_programs(1) - 1
    gate = mod_ref[5:6, :]

    def hidden():
        a = jnp.dot(h_ref[...], w1_ref[...].astype(BF16), preferred_element_type=F32)
        a = jnp.maximum(a, 0.0)
        return (a * a).astype(BF16)

    def down(a2):
        return jnp.dot(a2, w2_ref[...].astype(BF16), preferred_element_type=F32)

    @pl.when(k == 0)
    def _():
        h_ref[...] = (x_ref[...] * (1.0 + mod_ref[4:5, :]) + mod_ref[3:4, :]).astype(BF16)
        o_ref[...] = ALPHA * x_ref[...] + gate * down(hidden())

    @pl.when(jnp.logical_and(k > 0, k < last))
    def _():
        o_ref[...] += gate * down(hidden())

    @pl.when(k == last)
    def _():
        a2 = hidden()
        blocks = [slice(r0, r0 + FFN_TAIL_ROWS) for r0 in range(0, o_ref.shape[0], FFN_TAIL_ROWS)]

        def project(rows):
            o_ref[rows, :] += gate * down(a2[rows, :])

        project(blocks[0])
        for done, rows in zip(blocks, blocks[1:]):
            project(rows)
            o_ref[done, :] = _layer_norm(o_ref[done, :], ln_g_ref[...], ln_b_ref[...])
        o_ref[blocks[-1], :] = _layer_norm(o_ref[blocks[-1], :], ln_g_ref[...], ln_b_ref[...])


def _ffn_call(x, mod, w1_layers, w2_layers, layer, ln_g, ln_b):
    batch, seq, d = x.shape
    n_chunks = w1_layers.shape[2] // FFN_COLS
    tm = FFN_ROWS
    tiles_per_seq = seq // tm
    x_map = lambda i, k: (i // tiles_per_seq, i % tiles_per_seq, 0)
    return pl.pallas_call(
        _ffn_kernel,
        grid=(batch * tiles_per_seq, n_chunks),
        in_specs=[
            pl.BlockSpec((None, tm, d), x_map),
            pl.BlockSpec((None, N_MOD, d), lambda i, k: (i // tiles_per_seq, 0, 0)),
            pl.BlockSpec((None, d, FFN_COLS), lambda i, k: (layer, 0, k)),
            pl.BlockSpec((None, FFN_COLS, d), lambda i, k: (layer, k, 0)),
            pl.BlockSpec((1, d), lambda i, k: (0, 0)),
            pl.BlockSpec((1, d), lambda i, k: (0, 0)),
        ],
        out_specs=pl.BlockSpec((None, tm, d), x_map),
        out_shape=jax.ShapeDtypeStruct(x.shape, F32),
        scratch_shapes=[pltpu.VMEM((tm, d), BF16)],
        compiler_params=pltpu.CompilerParams(
            dimension_semantics=("arbitrary", "arbitrary"), vmem_limit_bytes=FFN_VMEM_LIMIT_BYTES),
        name="ffn",
    )(x, mod, w1_layers, w2_layers, ln_g.reshape(1, d), ln_b.reshape(1, d))


def kernel(x, c, w_mod, b_mod, w_in, w_pool, ls_pool, conv_w, conv_b, ln_conv_g, ln_conv_b,
           w_sgu, b_sgu, ln_sgu_g, ln_sgu_b, w_out, ln_mix_g, ln_mix_b, w_ff1, w_ff2, ln_ff_g, ln_ff_b):
    batch = x.shape[0]
    mod = _mod_call(c, w_mod, b_mod).reshape(DEPTH, batch, N_MOD, D_MODEL)
    for l in range(DEPTH):
        b_sgu_rows = jnp.repeat(b_sgu[l].T, SGU_HEAD_DIM, axis=1)
        x = _mixer_call(x, mod[l], w_in[l].astype(BF16), w_pool[l].astype(BF16), ls_pool[l],
                        conv_w[l], conv_b[l], ln_conv_g[l], ln_conv_b[l],
                        w_sgu[l], b_sgu_rows, ln_sgu_g[l], ln_sgu_b[l],
                        w_out[l].astype(BF16), ln_mix_g[l], ln_mix_b[l])
        x = _ffn_call(x, mod[l], w_ff1, w_ff2, l, ln_ff_g[l], ln_ff_b[l])
    return x
```

```python
import jax
import jax.numpy as jnp
from jax import lax
from jax.experimental import pallas as pl
from jax.experimental.pallas import tpu as pltpu

D_MODEL = 2048
DEPTH = 2
WIDTH_POOL = D_MODEL // 4
WIDTH_CONV = (3 * D_MODEL) // 8
WIDTH_SGU = D_MODEL - WIDTH_POOL - WIDTH_CONV
POOL_WINDOWS = (2, 4, 8, 16)
POOL_GROUP_DIM = WIDTH_POOL // len(POOL_WINDOWS)
CONV_WIDTH = 31
N_SGU_HEADS = 6
SGU_HEAD_DIM = WIDTH_SGU // N_SGU_HEADS
CHUNK = 128
D_FF = 4 * D_MODEL
IN_WIDTH = WIDTH_POOL + 2 * WIDTH_CONV + 2 * WIDTH_SGU
N_MOD = 6
LN_EPS = 1e-5
ALPHA = (2.0 * DEPTH) ** 0.25

OFF_VAL = WIDTH_POOL
OFF_GATE = OFF_VAL + WIDTH_CONV
OFF_Z = OFF_GATE + WIDTH_CONV

V7X_VMEM_BYTES = 64 * 1024 * 1024
VMEM_LIMIT_BYTES = V7X_VMEM_BYTES - 8 * 1024 * 1024
FFN_VMEM_LIMIT_BYTES = V7X_VMEM_BYTES - 2 * 1024 * 1024
LANES = 128
SUBLANES = 8

ROW_STRIDE = 4
ROW_GROUP = ROW_STRIDE * SUBLANES

HALO = 32
MIX_ROWS = 256
LN_ROWS = 16
CONV_ROWS = 2 * ROW_GROUP
FFN_ROWS = 1024
FFN_COLS = 512
FFN_TAIL_ROWS = 256
MOD_COLS = 1024

N_POOL_SLABS = WIDTH_POOL // LANES
N_CONV_SLABS = WIDTH_CONV // LANES

F32 = jnp.float32
BF16 = jnp.bfloat16


def _layer_norm(x, g, b):
    mu = jnp.mean(x, axis=-1, keepdims=True)
    xc = x - mu
    var = jnp.mean(xc * xc, axis=-1, keepdims=True)
    return xc * lax.rsqrt(var + LN_EPS) * g + b


def _strided_rows(start):
    return pl.ds(start, SUBLANES, stride=ROW_STRIDE)


def _mod_kernel(c_ref, w_ref, b_ref, o_ref):
    c = c_ref[...]
    c_act = (c * jax.nn.sigmoid(c)).astype(BF16)
    w = w_ref[...].astype(BF16)
    o_ref[...] = jnp.dot(c_act, w, preferred_element_type=F32) + b_ref[...]


def _mod_call(c, w_mod, b_mod):
    depth, d, n = w_mod.shape
    batch = c.shape[0]
    return pl.pallas_call(
        _mod_kernel,
        grid=(depth, n // MOD_COLS),
        in_specs=[
            pl.BlockSpec((batch, d), lambda l, j: (0, 0)),
            pl.BlockSpec((None, d, MOD_COLS), lambda l, j: (l, 0, j)),
            pl.BlockSpec((None, 1, MOD_COLS), lambda l, j: (l, 0, j)),
        ],
        out_specs=pl.BlockSpec((None, batch, MOD_COLS), lambda l, j: (l, 0, j)),
        out_shape=jax.ShapeDtypeStruct((depth, batch, n), F32),
        compiler_params=pltpu.CompilerParams(
            dimension_semantics=("arbitrary", "arbitrary"), vmem_limit_bytes=VMEM_LIMIT_BYTES),
        name="mod",
    )(c, w_mod, b_mod.reshape(depth, 1, n))


def _mixer_kernel(x_ref, mod_ref, w_in_ref, w_pool_ref, ls_pool_ref, conv_w_ref, conv_b_ref,
                  ln_conv_g_ref, ln_conv_b_ref, w_sgu_ref, b_sgu_ref, ln_sgu_g_ref, ln_sgu_b_ref,
                  w_out_ref, ln_g_ref, ln_b_ref, o_ref, a_hist, g_hist, pool_stage, conv_stage, ycat):
    ts = x_ref.shape[0]
    j = pl.program_id(1)

    @pl.when(j == 0)
    def _():
        a_hist[:, 0:HALO, :] = jnp.zeros((N_POOL_SLABS, HALO, LANES), F32)
        g_hist[:, 0:HALO, :] = jnp.zeros((N_CONV_SLABS, HALO, LANES), F32)

    shift = mod_ref[0:1, :]
    scale = mod_ref[1:2, :]
    gate = mod_ref[2:3, :]
    h = (x_ref[...] * (1.0 + scale) + shift).astype(BF16)

    vg = jnp.dot(h, w_in_ref[:, OFF_VAL:OFF_Z], preferred_element_type=F32)
    for s in range(N_CONV_SLABS):
        val = vg[:, 2 * s * LANES:(2 * s + 1) * LANES]
        gt = vg[:, (2 * s + 1) * LANES:(2 * s + 2) * LANES]
        g_hist[s, HALO:HALO + ts, :] = val * jax.nn.sigmoid(gt)
        for r0 in range(0, ts, CONV_ROWS):
            starts = [r0 + g * ROW_GROUP + q for g in range(CONV_ROWS // ROW_GROUP) for q in range(ROW_STRIDE)]
            acc = [jnp.broadcast_to(conv_b_ref[s], (SUBLANES, LANES)) for _ in starts]
            for k in range(CONV_WIDTH):
                back = CONV_WIDTH - 1 - k
                w_k = conv_w_ref[s, k:k + 1, :]
                for n, start in enumerate(starts):
                    acc[n] = acc[n] + w_k * g_hist[s, _strided_rows(HALO + start - back), :]
            for n, start in enumerate(starts):
                conv_stage[s, _strided_rows(start), :] = acc[n]
    g_hist[:, 0:HALO, :] = g_hist[:, ts:ts + HALO, :]

    a = jnp.dot(h, w_in_ref[:, 0:WIDTH_POOL], preferred_element_type=F32)
    for s in range(N_POOL_SLABS):
        a_hist[s, HALO:HALO + ts, :] = a[:, s * LANES:(s + 1) * LANES]
    row_in_group = lax.broadcasted_iota(jnp.int32, (SUBLANES, LANES), 0) * ROW_STRIDE
    for gi, win in enumerate(POOL_WINDOWS):
        for r0 in range(0, ts, ROW_GROUP):
            for q in range(ROW_STRIDE):
                cur = a_hist[gi, _strided_rows(HALO + r0 + q), :]
                wsum = cur
                for back in range(1, win):
                    wsum = wsum + a_hist[gi, _strided_rows(HALO + r0 + q - back), :]
                pos = row_in_group + (j * ts + (r0 + q + 1))
                count = jnp.minimum(pos, win).astype(F32)
                pool_stage[gi, _strided_rows(r0 + q), :] = wsum / count - cur
    a_hist[:, 0:HALO, :] = a_hist[:, ts:ts + HALO, :]
    for gi in range(N_POOL_SLABS):
        lanes = slice(gi * LANES, (gi + 1) * LANES)
        y = jnp.dot(pool_stage[gi].astype(BF16), w_pool_ref[gi], preferred_element_type=F32)
        ycat[:, lanes] = (y * ls_pool_ref[:, lanes]).astype(BF16)

    z = jnp.dot(h, w_in_ref[:, OFF_Z:IN_WIDTH], preferred_element_type=F32)
    z = 0.5 * z * (1.0 + lax.erf(z * (2.0 ** -0.5)))
    u = z[:, 0:WIDTH_SGU]
    v = _layer_norm(z[:, WIDTH_SGU:], ln_sgu_g_ref[...], ln_sgu_b_ref[...]).astype(BF16)
    n_chunks = ts // CHUNK
    row = lax.broadcasted_iota(jnp.int32, (CHUNK, CHUNK), 0)
    col = lax.broadcasted_iota(jnp.int32, (CHUNK, CHUNK), 1)
    causal = col <= row
    for hd in range(N_SGU_HEADS):
        lanes = slice(hd * SGU_HEAD_DIM, (hd + 1) * SGU_HEAD_DIM)
        w_h = jnp.where(causal, w_sgu_ref[hd], 0.0).astype(BF16)
        v_h = jnp.concatenate([v[n * CHUNK:(n + 1) * CHUNK, lanes] for n in range(n_chunks)], axis=1)
        m_h = jnp.dot(w_h, v_h, preferred_element_type=F32)
        for n in range(n_chunks):
            rows = slice(n * CHUNK, (n + 1) * CHUNK)
            mixed = m_h[:, n * SGU_HEAD_DIM:(n + 1) * SGU_HEAD_DIM] + b_sgu_ref[:, lanes]
            ycat[rows, OFF_GATE + hd * SGU_HEAD_DIM:OFF_GATE + (hd + 1) * SGU_HEAD_DIM] = (
                u[rows, lanes] * mixed).astype(BF16)

    def out_proj(lo, hi):
        return jnp.dot(ycat[:, lo:hi], w_out_ref[lo:hi, :], preferred_element_type=F32)

    y = out_proj(0, OFF_VAL) + out_proj(OFF_GATE, D_MODEL)
    o_ref[...] = ALPHA * x_ref[...] + gate * y

    for r0 in range(0, ts, LN_ROWS):
        rows = slice(r0, r0 + LN_ROWS)
        pre = [conv_stage[s, rows, :] for s in range(N_CONV_SLABS)]
        tot = pre[0]
        for s in range(1, N_CONV_SLABS):
            tot = tot + pre[s]
        mu = jnp.sum(tot, axis=-1, keepdims=True) / WIDTH_CONV
        xc = [p - mu for p in pre]
        sq = xc[0] * xc[0]
        for s in range(1, N_CONV_SLABS):
            sq = sq + xc[s] * xc[s]
        rstd = lax.rsqrt(jnp.sum(sq, axis=-1, keepdims=True) / WIDTH_CONV + LN_EPS)
        for s in range(N_CONV_SLABS):
            hn = xc[s] * rstd * ln_conv_g_ref[s] + ln_conv_b_ref[s]
            ycat[rows, OFF_VAL + s * LANES:OFF_VAL + (s + 1) * LANES] = (hn * jax.nn.sigmoid(hn)).astype(BF16)

    o_ref[...] += gate * out_proj(OFF_VAL, OFF_GATE)
    for r0 in range(0, ts, LN_ROWS):
        rows = slice(r0, r0 + LN_ROWS)
        o_ref[rows, :] = _layer_norm(o_ref[rows, :], ln_g_ref[...], ln_b_ref[...])


def _mixer_call(x, mod, w_in, w_pool, ls_pool, conv_w, conv_b, ln_conv_g, ln_conv_b,
                w_sgu, b_sgu_rows, ln_sgu_g, ln_sgu_b, w_out, ln_g, ln_b):
    batch, seq, d = x.shape
    ts = MIX_ROWS
    const2 = lambda b, j: (0, 0)
    const3 = lambda b, j: (0, 0, 0)
    resident = pl.Buffered(1)
    row = lambda v: v.reshape(1, -1)
    slab_rows = lambda v: v.reshape(N_CONV_SLABS, 1, LANES)
    conv_w_slabs = conv_w.reshape(CONV_WIDTH, N_CONV_SLABS, LANES).transpose(1, 0, 2)
    return pl.pallas_call(
        _mixer_kernel,
        grid=(batch, seq // ts),
        in_specs=[
            pl.BlockSpec((None, ts, d), lambda b, j: (b, j, 0)),
            pl.BlockSpec((None, N_MOD, d), lambda b, j: (b, 0, 0)),
            pl.BlockSpec(w_in.shape, const2, pipeline_mode=resident),
            pl.BlockSpec(w_pool.shape, const3),
            pl.BlockSpec((1, WIDTH_POOL), const2),
            pl.BlockSpec((N_CONV_SLABS, CONV_WIDTH, LANES), const3),
            pl.BlockSpec((N_CONV_SLABS, 1, LANES), const3),
            pl.BlockSpec((N_CONV_SLABS, 1, LANES), const3),
            pl.BlockSpec((N_CONV_SLABS, 1, LANES), const3),
            pl.BlockSpec(w_sgu.shape, const3),
            pl.BlockSpec(b_sgu_rows.shape, const2),
            pl.BlockSpec((1, WIDTH_SGU), const2),
            pl.BlockSpec((1, WIDTH_SGU), const2),
            pl.BlockSpec(w_out.shape, const2, pipeline_mode=resident),
            pl.BlockSpec((1, d), const2),
            pl.BlockSpec((1, d), const2),
        ],
        out_specs=pl.BlockSpec((None, ts, d), lambda b, j: (b, j, 0)),
        out_shape=jax.ShapeDtypeStruct(x.shape, F32),
        scratch_shapes=[
            pltpu.VMEM((N_POOL_SLABS, HALO + ts, LANES), F32),
            pltpu.VMEM((N_CONV_SLABS, HALO + ts, LANES), F32),
            pltpu.VMEM((N_POOL_SLABS, ts, LANES), F32),
            pltpu.VMEM((N_CONV_SLABS, ts, LANES), F32),
            pltpu.VMEM((ts, d), BF16),
        ],
        compiler_params=pltpu.CompilerParams(
            dimension_semantics=("arbitrary", "arbitrary"), vmem_limit_bytes=VMEM_LIMIT_BYTES),
        name="mixer",
    )(x, mod, w_in, w_pool, row(ls_pool), conv_w_slabs, slab_rows(conv_b), slab_rows(ln_conv_g),
      slab_rows(ln_conv_b), w_sgu, b_sgu_rows, row(ln_sgu_g), row(ln_sgu_b), w_out, row(ln_g), row(ln_b))


def _ffn_kernel(x_ref, mod_ref, w1_ref, w2_ref, ln_g_ref, ln_b_ref, o_ref, h_ref):
    k = pl.program_id(1)
    last = pl.num_programs(1) - 1
    gate = mod_ref[5:6, :]

    def hidden():
        a = jnp.dot(h_ref[...], w1_ref[...].astype(BF16), preferred_element_type=F32)
        a = jnp.maximum(a, 0.0)
        return (a * a).astype(BF16)

    def down(a2):
        return jnp.dot(a2, w2_ref[...].astype(BF16), preferred_element_type=F32)

    @pl.when(k == 0)
    def _():
        h_ref[...] = (x_ref[...] * (1.0 + mod_ref[4:5, :]) + mod_ref[3:4, :]).astype(BF16)
        o_ref[...] = ALPHA * x_ref[...] + gate * down(hidden())

    @pl.when(jnp.logical_and(k > 0, k < last))
    def _():
        o_ref[...] += gate * down(hidden())

    @pl.when(k == last)
    def _():
        a2 = hidden()
        blocks = [slice(r0, r0 + FFN_TAIL_ROWS) for r0 in range(0, o_ref.shape[0], FFN_TAIL_ROWS)]

        def project(rows):
            o_ref[rows, :] += gate * down(a2[rows, :])

        project(blocks[0])
        for done, rows in zip(blocks, blocks[1:]):
            project(rows)
            o_ref[done, :] = _layer_norm(o_ref[done, :], ln_g_ref[...], ln_b_ref[...])
        o_ref[blocks[-1], :] = _layer_norm(o_ref[blocks[-1], :], ln_g_ref[...], ln_b_ref[...])


def _ffn_call(x, mod, w1_layers, w2_layers, layer, ln_g, ln_b):
    batch, seq, d = x.shape
    n_chunks = w1_layers.shape[2] // FFN_COLS
    tm = FFN_ROWS
    tiles_per_seq = seq // tm
    x_map = lambda i, k: (i // tiles_per_seq, i % tiles_per_seq, 0)
    return pl.pallas_call(
        _ffn_kernel,
        grid=(batch * tiles_per_seq, n_chunks),
        in_specs=[
            pl.BlockSpec((None, tm, d), x_map),
            pl.BlockSpec((None, N_MOD, d), lambda i, k: (i // tiles_per_seq, 0, 0)),
            pl.BlockSpec((None, d, FFN_COLS), lambda i, k: (layer, 0, k)),
            pl.BlockSpec((None, FFN_COLS, d), lambda i, k: (layer, k, 0)),
            pl.BlockSpec((1, d), lambda i, k: (0, 0)),
            pl.BlockSpec((1, d), lambda i, k: (0, 0)),
        ],
        out_specs=pl.BlockSpec((None, tm, d), x_map),
        out_shape=jax.ShapeDtypeStruct(x.shape, F32),
        scratch_shapes=[pltpu.VMEM((tm, d), BF16)],
        compiler_params=pltpu.CompilerParams(
            dimension_semantics=("arbitrary", "arbitrary"), vmem_limit_bytes=FFN_VMEM_LIMIT_BYTES),
        name="ffn",
    )(x, mod, w1_layers, w2_layers, ln_g.reshape(1, d), ln_b.reshape(1, d))


def kernel(x, c, w_mod, b_mod, w_in, w_pool, ls_pool, conv_w, conv_b, ln_conv_g, ln_conv_b,
           w_sgu, b_sgu, ln_sgu_g, ln_sgu_b, w_out, ln_mix_g, ln_mix_b, w_ff1, w_ff2, ln_ff_g, ln_ff_b):
    batch = x.shape[0]
    mod = _mod_call(c, w_mod, b_mod).reshape(DEPTH, batch, N_MOD, D_MODEL)
    for l in range(DEPTH):
        b_sgu_rows = jnp.repeat(b_sgu[l].T, SGU_HEAD_DIM, axis=1)
        w_val = w_in[l, :, OFF_VAL:OFF_GATE].reshape(D_MODEL, N_CONV_SLABS, 1, LANES)
        w_gate = w_in[l, :, OFF_GATE:OFF_Z].reshape(D_MODEL, N_CONV_SLABS, 1, LANES)
        w_vg = jnp.concatenate([w_val, w_gate], axis=2).reshape(D_MODEL, 2 * WIDTH_CONV)
        w_in_l = jnp.concatenate([w_in[l, :, 0:OFF_VAL], w_vg, w_in[l, :, OFF_Z:]], axis=1).astype(BF16)
        x = _mixer_call(x, mod[l], w_in_l, w_pool[l].astype(BF16), ls_pool[l],
                        conv_w[l], conv_b[l], ln_conv_g[l], ln_conv_b[l],
                        w_sgu[l], b_sgu_rows, ln_sgu_g[l], ln_sgu_b[l],
                        w_out[l].astype(BF16), ln_mix_g[l], ln_mix_b[l])
        x = _ffn_call(x, mod[l], w_ff1, w_ff2, l, ln_ff_g[l], ln_ff_b[l])
    return x
```

```python
import jax
import jax.numpy as jnp
from jax import lax
from jax.experimental import pallas as pl
from jax.experimental.pallas import tpu as pltpu

D_MODEL = 2048
DEPTH = 2
WIDTH_POOL = D_MODEL // 4
WIDTH_CONV = (3 * D_MODEL) // 8
WIDTH_SGU = D_MODEL - WIDTH_POOL - WIDTH_CONV
POOL_WINDOWS = (2, 4, 8, 16)
POOL_GROUP_DIM = WIDTH_POOL // len(POOL_WINDOWS)
CONV_WIDTH = 31
N_SGU_HEADS = 6
SGU_HEAD_DIM = WIDTH_SGU // N_SGU_HEADS
CHUNK = 128
D_FF = 4 * D_MODEL
IN_WIDTH = WIDTH_POOL + 2 * WIDTH_CONV + 2 * WIDTH_SGU
N_MOD = 6
LN_EPS = 1e-5
ALPHA = (2.0 * DEPTH) ** 0.25

OFF_VAL = WIDTH_POOL
OFF_GATE = OFF_VAL + WIDTH_CONV
OFF_Z = OFF_GATE + WIDTH_CONV

V7X_VMEM_BYTES = 64 * 1024 * 1024
VMEM_LIMIT_BYTES = V7X_VMEM_BYTES - 8 * 1024 * 1024
FFN_VMEM_LIMIT_BYTES = V7X_VMEM_BYTES - 2 * 1024 * 1024
LANES = 128
SUBLANES = 8

ROW_STRIDE = 4
ROW_GROUP = ROW_STRIDE * SUBLANES

HALO = 32
MIX_ROWS = 256
LN_ROWS = 16
CONV_ROWS = 2 * ROW_GROUP
CAST_COLS = 2 * LANES
FFN_ROWS = 1024
FFN_COLS = 512
FFN_TAIL_ROWS = 256
MOD_COLS = 1024

N_POOL_SLABS = WIDTH_POOL // LANES
N_CONV_SLABS = WIDTH_CONV // LANES

F32 = jnp.float32
BF16 = jnp.bfloat16


def _layer_norm(x, g, b):
    mu = jnp.mean(x, axis=-1, keepdims=True)
    xc = x - mu
    var = jnp.mean(xc * xc, axis=-1, keepdims=True)
    return xc * lax.rsqrt(var + LN_EPS) * g + b


def _strided_rows(start):
    return pl.ds(start, SUBLANES, stride=ROW_STRIDE)


def _mod_kernel(c_ref, w_ref, b_ref, o_ref):
    c = c_ref[...]
    c_act = (c * jax.nn.sigmoid(c)).astype(BF16)
    w = w_ref[...].astype(BF16)
    o_ref[...] = jnp.dot(c_act, w, preferred_element_type=F32) + b_ref[...]


def _mod_call(c, w_mod, b_mod):
    depth, d, n = w_mod.shape
    batch = c.shape[0]
    return pl.pallas_call(
        _mod_kernel,
        grid=(depth, n // MOD_COLS),
        in_specs=[
            pl.BlockSpec((batch, d), lambda l, j: (0, 0)),
            pl.BlockSpec((None, d, MOD_COLS), lambda l, j: (l, 0, j)),
            pl.BlockSpec((None, 1, MOD_COLS), lambda l, j: (l, 0, j)),
        ],
        out_specs=pl.BlockSpec((None, batch, MOD_COLS), lambda l, j: (l, 0, j)),
        out_shape=jax.ShapeDtypeStruct((depth, batch, n), F32),
        compiler_params=pltpu.CompilerParams(
            dimension_semantics=("arbitrary", "arbitrary"), vmem_limit_bytes=VMEM_LIMIT_BYTES),
        name="mod",
    )(c, w_mod, b_mod.reshape(depth, 1, n))


def _cast_kernel(a_ref, b_ref, o_ref):
    o_ref[:, 0:LANES] = a_ref[...].astype(BF16)
    o_ref[:, LANES:CAST_COLS] = b_ref[...].astype(BF16)


def _cast_call(w_layers, layer, src_block):
    _, d, n = w_layers.shape
    halves = CAST_COLS // LANES
    return pl.pallas_call(
        _cast_kernel,
        grid=(n // CAST_COLS,),
        in_specs=[
            pl.BlockSpec((None, d, LANES), lambda j: (layer, 0, src_block(halves * j))),
            pl.BlockSpec((None, d, LANES), lambda j: (layer, 0, src_block(halves * j + 1))),
        ],
        out_specs=pl.BlockSpec((d, CAST_COLS), lambda j: (0, j)),
        out_shape=jax.ShapeDtypeStruct((d, n), BF16),
        compiler_params=pltpu.CompilerParams(dimension_semantics=("arbitrary",)),
        name="cast",
    )(w_layers, w_layers)


def _interleaved_in_block(j):
    first, n = OFF_VAL // LANES, N_CONV_SLABS
    k = j - first
    return jnp.where(jnp.logical_and(k >= 0, k < 2 * n), first + k // 2 + n * (k % 2), j)


def _mixer_kernel(x_ref, mod_ref, w_in_ref, w_pool_ref, ls_pool_ref, conv_w_ref, conv_b_ref,
                  ln_conv_g_ref, ln_conv_b_ref, w_sgu_ref, b_sgu_ref, ln_sgu_g_ref, ln_sgu_b_ref,
                  w_out_ref, ln_g_ref, ln_b_ref, o_ref, a_hist, g_hist, pool_stage, conv_stage, ycat):
    ts = x_ref.shape[0]
    j = pl.program_id(1)

    @pl.when(j == 0)
    def _():
        a_hist[:, 0:HALO, :] = jnp.zeros((N_POOL_SLABS, HALO, LANES), F32)
        g_hist[:, 0:HALO, :] = jnp.zeros((N_CONV_SLABS, HALO, LANES), F32)

    shift = mod_ref[0:1, :]
    scale = mod_ref[1:2, :]
    gate = mod_ref[2:3, :]
    h = (x_ref[...] * (1.0 + scale) + shift).astype(BF16)

    vg = jnp.dot(h, w_in_ref[:, OFF_VAL:OFF_Z], preferred_element_type=F32)
    for s in range(N_CONV_SLABS):
        val = vg[:, 2 * s * LANES:(2 * s + 1) * LANES]
        gt = vg[:, (2 * s + 1) * LANES:(2 * s + 2) * LANES]
        g_hist[s, HALO:HALO + ts, :] = val * jax.nn.sigmoid(gt)
        for r0 in range(0, ts, CONV_ROWS):
            starts = [r0 + g * ROW_GROUP + q for g in range(CONV_ROWS // ROW_GROUP) for q in range(ROW_STRIDE)]
            acc = [jnp.broadcast_to(conv_b_ref[s], (SUBLANES, LANES)) for _ in starts]
            for k in range(CONV_WIDTH):
                back = CONV_WIDTH - 1 - k
                w_k = conv_w_ref[s, k:k + 1, :]
                for n, start in enumerate(starts):
                    acc[n] = acc[n] + w_k * g_hist[s, _strided_rows(HALO + start - back), :]
            for n, start in enumerate(starts):
                conv_stage[s, _strided_rows(start), :] = acc[n]
    g_hist[:, 0:HALO, :] = g_hist[:, ts:ts + HALO, :]

    a = jnp.dot(h, w_in_ref[:, 0:WIDTH_POOL], preferred_element_type=F32)
    for s in range(N_POOL_SLABS):
        a_hist[s, HALO:HALO + ts, :] = a[:, s * LANES:(s + 1) * LANES]
    row_in_group = lax.broadcasted_iota(jnp.int32, (SUBLANES, LANES), 0) * ROW_STRIDE
    for gi, win in enumerate(POOL_WINDOWS):
        for r0 in range(0, ts, ROW_GROUP):
            for q in range(ROW_STRIDE):
                cur = a_hist[gi, _strided_rows(HALO + r0 + q), :]
                wsum = cur
                for back in range(1, win):
                    wsum = wsum + a_hist[gi, _strided_rows(HALO + r0 + q - back), :]
                pos = row_in_group + (j * ts + (r0 + q + 1))
                count = jnp.minimum(pos, win).astype(F32)
                pool_stage[gi, _strided_rows(r0 + q), :] = wsum / count - cur
    a_hist[:, 0:HALO, :] = a_hist[:, ts:ts + HALO, :]
    for gi in range(N_POOL_SLABS):
        lanes = slice(gi * LANES, (gi + 1) * LANES)
        y = jnp.dot(pool_stage[gi].astype(BF16), w_pool_ref[gi], preferred_element_type=F32)
        ycat[:, lanes] = (y * ls_pool_ref[:, lanes]).astype(BF16)

    z = jnp.dot(h, w_in_ref[:, OFF_Z:IN_WIDTH], preferred_element_type=F32)
    z = 0.5 * z * (1.0 + lax.erf(z * (2.0 ** -0.5)))
    u = z[:, 0:WIDTH_SGU]
    v = _layer_norm(z[:, WIDTH_SGU:], ln_sgu_g_ref[...], ln_sgu_b_ref[...]).astype(BF16)
    n_chunks = ts // CHUNK
    row = lax.broadcasted_iota(jnp.int32, (CHUNK, CHUNK), 0)
    col = lax.broadcasted_iota(jnp.int32, (CHUNK, CHUNK), 1)
    causal = col <= row
    for hd in range(N_SGU_HEADS):
        lanes = slice(hd * SGU_HEAD_DIM, (hd + 1) * SGU_HEAD_DIM)
        w_h = jnp.where(causal, w_sgu_ref[hd], 0.0).astype(BF16)
        v_h = jnp.concatenate([v[n * CHUNK:(n + 1) * CHUNK, lanes] for n in range(n_chunks)], axis=1)
        m_h = jnp.dot(w_h, v_h, preferred_element_type=F32)
        for n in range(n_chunks):
            rows = slice(n * CHUNK, (n + 1) * CHUNK)
            mixed = m_h[:, n * SGU_HEAD_DIM:(n + 1) * SGU_HEAD_DIM] + b_sgu_ref[:, lanes]
            ycat[rows, OFF_GATE + hd * SGU_HEAD_DIM:OFF_GATE + (hd + 1) * SGU_HEAD_DIM] = (
                u[rows, lanes] * mixed).astype(BF16)

    def out_proj(lo, hi):
        return jnp.dot(ycat[:, lo:hi], w_out_ref[lo:hi, :], preferred_element_type=F32)

    y = out_proj(0, OFF_VAL) + out_proj(OFF_GATE, D_MODEL)
    o_ref[...] = ALPHA * x_ref[...] + gate * y

    for r0 in range(0, ts, LN_ROWS):
        rows = slice(r0, r0 + LN_ROWS)
        pre = [conv_stage[s, rows, :] for s in range(N_CONV_SLABS)]
        tot = pre[0]
        for s in range(1, N_CONV_SLABS):
            tot = tot + pre[s]
        mu = jnp.sum(tot, axis=-1, keepdims=True) / WIDTH_CONV
        xc = [p - mu for p in pre]
        sq = xc[0] * xc[0]
        for s in range(1, N_CONV_SLABS):
            sq = sq + xc[s] * xc[s]
        rstd = lax.rsqrt(jnp.sum(sq, axis=-1, keepdims=True) / WIDTH_CONV + LN_EPS)
        for s in range(N_CONV_SLABS):
            hn = xc[s] * rstd * ln_conv_g_ref[s] + ln_conv_b_ref[s]
            ycat[rows, OFF_VAL + s * LANES:OFF_VAL + (s + 1) * LANES] = (hn * jax.nn.sigmoid(hn)).astype(BF16)

    o_ref[...] += gate * out_proj(OFF_VAL, OFF_GATE)
    for r0 in range(0, ts, LN_ROWS):
        rows = slice(r0, r0 + LN_ROWS)
        o_ref[rows, :] = _layer_norm(o_ref[rows, :], ln_g_ref[...], ln_b_ref[...])


def _mixer_call(x, mod, w_in, w_pool, ls_pool, conv_w, conv_b, ln_conv_g, ln_conv_b,
                w_sgu, b_sgu_rows, ln_sgu_g, ln_sgu_b, w_out, ln_g, ln_b):
    batch, seq, d = x.shape
    ts = MIX_ROWS
    const2 = lambda b, j: (0, 0)
    const3 = lambda b, j: (0, 0, 0)
    resident = pl.Buffered(1)
    row = lambda v: v.reshape(1, -1)
    slab_rows = lambda v: v.reshape(N_CONV_SLABS, 1, LANES)
    conv_w_slabs = conv_w.reshape(CONV_WIDTH, N_CONV_SLABS, LANES).transpose(1, 0, 2)
    return pl.pallas_call(
        _mixer_kernel,
        grid=(batch, seq // ts),
        in_specs=[
            pl.BlockSpec((None, ts, d), lambda b, j: (b, j, 0)),
            pl.BlockSpec((None, N_MOD, d), lambda b, j: (b, 0, 0)),
            pl.BlockSpec(w_in.shape, const2, pipeline_mode=resident),
            pl.BlockSpec(w_pool.shape, const3),
            pl.BlockSpec((1, WIDTH_POOL), const2),
            pl.BlockSpec((N_CONV_SLABS, CONV_WIDTH, LANES), const3),
            pl.BlockSpec((N_CONV_SLABS, 1, LANES), const3),
            pl.BlockSpec((N_CONV_SLABS, 1, LANES), const3),
            pl.BlockSpec((N_CONV_SLABS, 1, LANES), const3),
            pl.BlockSpec(w_sgu.shape, const3),
            pl.BlockSpec(b_sgu_rows.shape, const2),
            pl.BlockSpec((1, WIDTH_SGU), const2),
            pl.BlockSpec((1, WIDTH_SGU), const2),
            pl.BlockSpec(w_out.shape, const2, pipeline_mode=resident),
            pl.BlockSpec((1, d), const2),
            pl.BlockSpec((1, d), const2),
        ],
        out_specs=pl.BlockSpec((None, ts, d), lambda b, j: (b, j, 0)),
        out_shape=jax.ShapeDtypeStruct(x.shape, F32),
        scratch_shapes=[
            pltpu.VMEM((N_POOL_SLABS, HALO + ts, LANES), F32),
            pltpu.VMEM((N_CONV_SLABS, HALO + ts, LANES), F32),
            pltpu.VMEM((N_POOL_SLABS, ts, LANES), F32),
            pltpu.VMEM((N_CONV_SLABS, ts, LANES), F32),
            pltpu.VMEM((ts, d), BF16),
        ],
        compiler_params=pltpu.CompilerParams(
            dimension_semantics=("arbitrary", "arbitrary"), vmem_limit_bytes=VMEM_LIMIT_BYTES),
        name="mixer",
    )(x, mod, w_in, w_pool, row(ls_pool), conv_w_slabs, slab_rows(conv_b), slab_rows(ln_conv_g),
      slab_rows(ln_conv_b), w_sgu, b_sgu_rows, row(ln_sgu_g), row(ln_sgu_b), w_out, row(ln_g), row(ln_b))


def _ffn_kernel(x_ref, mod_ref, w1_ref, w2_ref, ln_g_ref, ln_b_ref, o_ref, h_ref):
    k = pl.program_id(1)
    last = pl.num_programs(1) - 1
    gate = mod_ref[5:6, :]

    def hidden():
        a = jnp.dot(h_ref[...], w1_ref[...].astype(BF16), preferred_element_type=F32)
        a = jnp.maximum(a, 0.0)
        return (a * a).astype(BF16)

    def down(a2):
        return jnp.dot(a2, w2_ref[...].astype(BF16), preferred_element_type=F32)

    @pl.when(k == 0)
    def _():
        h_ref[...] = (x_ref[...] * (1.0 + mod_ref[4:5, :]) + mod_ref[3:4, :]).astype(BF16)
        o_ref[...] = ALPHA * x_ref[...] + gate * down(hidden())

    @pl.when(jnp.logical_and(k > 0, k < last))
    def _():
        o_ref[...] += gate * down(hidden())

    @pl.when(k == last)
    def _():
        a2 = hidden()
        blocks = [slice(r0, r0 + FFN_TAIL_ROWS) for r0 in range(0, o_ref.shape[0], FFN_TAIL_ROWS)]

        def project(rows):
            o_ref[rows, :] += gate * down(a2[rows, :])

        project(blocks[0])
        for done, rows in zip(blocks, blocks[1:]):
            project(rows)
            o_ref[done, :] = _layer_norm(o_ref[done, :], ln_g_ref[...], ln_b_ref[...])
        o_ref[blocks[-1], :] = _layer_norm(o_ref[blocks[-1], :], ln_g_ref[...], ln_b_ref[...])


def _ffn_call(x, mod, w1_layers, w2_layers, layer, ln_g, ln_b):
    batch, seq, d = x.shape
    n_chunks = w1_layers.shape[2] // FFN_COLS
    tm = FFN_ROWS
    tiles_per_seq = seq // tm
    x_map = lambda i, k: (i // tiles_per_seq, i % tiles_per_seq, 0)
    return pl.pallas_call(
        _ffn_kernel,
        grid=(batch * tiles_per_seq, n_chunks),
        in_specs=[
            pl.BlockSpec((None, tm, d), x_map),
            pl.BlockSpec((None, N_MOD, d), lambda i, k: (i // tiles_per_seq, 0, 0)),
            pl.BlockSpec((None, d, FFN_COLS), lambda i, k: (layer, 0, k)),
            pl.BlockSpec((None, FFN_COLS, d), lambda i, k: (layer, k, 0)),
            pl.BlockSpec((1, d), lambda i, k: (0, 0)),
            pl.BlockSpec((1, d), lambda i, k: (0, 0)),
        ],
        out_specs=pl.BlockSpec((None, tm, d), x_map),
        out_shape=jax.ShapeDtypeStruct(x.shape, F32),
        scratch_shapes=[pltpu.VMEM((tm, d), BF16)],
        compiler_params=pltpu.CompilerParams(
            dimension_semantics=("arbitrary", "arbitrary"), vmem_limit_bytes=FFN_VMEM_LIMIT_BYTES),
        name="ffn",
    )(x, mod, w1_layers, w2_layers, ln_g.reshape(1, d), ln_b.reshape(1, d))


def kernel(x, c, w_mod, b_mod, w_in, w_pool, ls_pool, conv_w, conv_b, ln_conv_g, ln_conv_b,
           w_sgu, b_sgu, ln_sgu_g, ln_sgu_b, w_out, ln_mix_g, ln_mix_b, w_ff1, w_ff2, ln_ff_g, ln_ff_b):
    batch = x.shape[0]
    mod = _mod_call(c, w_mod, b_mod).reshape(DEPTH, batch, N_MOD, D_MODEL)
    for l in range(DEPTH):
        b_sgu_rows = jnp.repeat(b_sgu[l].T, SGU_HEAD_DIM, axis=1)
        w_in_l = _cast_call(w_in, l, _interleaved_in_block)
        w_out_l = _cast_call(w_out, l, lambda j: j)
        x = _mixer_call(x, mod[l], w_in_l, w_pool[l].astype(BF16), ls_pool[l],
                        conv_w[l], conv_b[l], ln_conv_g[l], ln_conv_b[l],
                        w_sgu[l], b_sgu_rows, ln_sgu_g[l], ln_sgu_b[l],
                        w_out_l, ln_mix_g[l], ln_mix_b[l])
        x = _ffn_call(x, mod[l], w_ff1, w_ff2, l, ln_ff_g[l], ln_ff_b[l])
    return x
```

```python
import jax
import jax.numpy as jnp
from jax import lax
from jax.experimental import pallas as pl
from jax.experimental.pallas import tpu as pltpu

D_MODEL = 2048
DEPTH = 2
WIDTH_POOL = D_MODEL // 4
WIDTH_CONV = (3 * D_MODEL) // 8
WIDTH_SGU = D_MODEL - WIDTH_POOL - WIDTH_CONV
POOL_WINDOWS = (2, 4, 8, 16)
POOL_GROUP_DIM = WIDTH_POOL // len(POOL_WINDOWS)
CONV_WIDTH = 31
N_SGU_HEADS = 6
SGU_HEAD_DIM = WIDTH_SGU // N_SGU_HEADS
CHUNK = 128
D_FF = 4 * D_MODEL
IN_WIDTH = WIDTH_POOL + 2 * WIDTH_CONV + 2 * WIDTH_SGU
N_MOD = 6
LN_EPS = 1e-5
ALPHA = (2.0 * DEPTH) ** 0.25

OFF_VAL = WIDTH_POOL
OFF_GATE = OFF_VAL + WIDTH_CONV
OFF_Z = OFF_GATE + WIDTH_CONV

V7X_VMEM_BYTES = 64 * 1024 * 1024
VMEM_LIMIT_BYTES = V7X_VMEM_BYTES - 8 * 1024 * 1024
FFN_VMEM_LIMIT_BYTES = V7X_VMEM_BYTES - 2 * 1024 * 1024
LANES = 128
SUBLANES = 8

ROW_STRIDE = 4
ROW_GROUP = ROW_STRIDE * SUBLANES

HALO = 32
MIX_ROWS = 512
LN_ROWS = 16
CONV_ROWS = 2 * ROW_GROUP
CAST_COLS = 2 * LANES
FFN_ROWS = 1024
FFN_COLS = 512
FFN_TAIL_ROWS = 256
MOD_COLS = 1024

N_POOL_SLABS = WIDTH_POOL // LANES
N_CONV_SLABS = WIDTH_CONV // LANES

F32 = jnp.float32
BF16 = jnp.bfloat16


def _layer_norm(x, g, b):
    mu = jnp.mean(x, axis=-1, keepdims=True)
    xc = x - mu
    var = jnp.mean(xc * xc, axis=-1, keepdims=True)
    return xc * lax.rsqrt(var + LN_EPS) * g + b


def _strided_rows(start):
    return pl.ds(start, SUBLANES, stride=ROW_STRIDE)


def _mod_kernel(c_ref, w_ref, b_ref, o_ref):
    c = c_ref[...]
    c_act = (c * jax.nn.sigmoid(c)).astype(BF16)
    w = w_ref[...].astype(BF16)
    o_ref[...] = jnp.dot(c_act, w, preferred_element_type=F32) + b_ref[...]


def _mod_call(c, w_mod, b_mod):
    depth, d, n = w_mod.shape
    batch = c.shape[0]
    return pl.pallas_call(
        _mod_kernel,
        grid=(depth, n // MOD_COLS),
        in_specs=[
            pl.BlockSpec((batch, d), lambda l, j: (0, 0)),
            pl.BlockSpec((None, d, MOD_COLS), lambda l, j: (l, 0, j)),
            pl.BlockSpec((None, 1, MOD_COLS), lambda l, j: (l, 0, j)),
        ],
        out_specs=pl.BlockSpec((None, batch, MOD_COLS), lambda l, j: (l, 0, j)),
        out_shape=jax.ShapeDtypeStruct((depth, batch, n), F32),
        compiler_params=pltpu.CompilerParams(
            dimension_semantics=("arbitrary", "arbitrary"), vmem_limit_bytes=VMEM_LIMIT_BYTES),
        name="mod",
    )(c, w_mod, b_mod.reshape(depth, 1, n))


def _cast_kernel(a_ref, b_ref, o_ref):
    o_ref[:, 0:LANES] = a_ref[...].astype(BF16)
    o_ref[:, LANES:CAST_COLS] = b_ref[...].astype(BF16)


def _cast_call(w_layers, layer, src_block):
    _, d, n = w_layers.shape
    halves = CAST_COLS // LANES
    return pl.pallas_call(
        _cast_kernel,
        grid=(n // CAST_COLS,),
        in_specs=[
            pl.BlockSpec((None, d, LANES), lambda j: (layer, 0, src_block(halves * j))),
            pl.BlockSpec((None, d, LANES), lambda j: (layer, 0, src_block(halves * j + 1))),
        ],
        out_specs=pl.BlockSpec((d, CAST_COLS), lambda j: (0, j)),
        out_shape=jax.ShapeDtypeStruct((d, n), BF16),
        compiler_params=pltpu.CompilerParams(dimension_semantics=("arbitrary",)),
        name="cast",
    )(w_layers, w_layers)


def _interleaved_in_block(j):
    first, n = OFF_VAL // LANES, N_CONV_SLABS
    k = j - first
    return jnp.where(jnp.logical_and(k >= 0, k < 2 * n), first + k // 2 + n * (k % 2), j)


def _mixer_kernel(x_ref, mod_ref, w_in_ref, w_pool_ref, ls_pool_ref, conv_w_ref, conv_b_ref,
                  ln_conv_g_ref, ln_conv_b_ref, w_sgu_ref, b_sgu_ref, ln_sgu_g_ref, ln_sgu_b_ref,
                  w_out_ref, ln_g_ref, ln_b_ref, o_ref, a_hist, g_hist, pool_stage, conv_stage, ycat):
    ts = x_ref.shape[0]
    j = pl.program_id(1)

    @pl.when(j == 0)
    def _():
        a_hist[:, 0:HALO, :] = jnp.zeros((N_POOL_SLABS, HALO, LANES), F32)
        g_hist[:, 0:HALO, :] = jnp.zeros((N_CONV_SLABS, HALO, LANES), F32)

    shift = mod_ref[0:1, :]
    scale = mod_ref[1:2, :]
    gate = mod_ref[2:3, :]
    h = (x_ref[...] * (1.0 + scale) + shift).astype(BF16)

    vg = jnp.dot(h, w_in_ref[:, OFF_VAL:OFF_Z], preferred_element_type=F32)
    for s in range(N_CONV_SLABS):
        val = vg[:, 2 * s * LANES:(2 * s + 1) * LANES]
        gt = vg[:, (2 * s + 1) * LANES:(2 * s + 2) * LANES]
        g_hist[s, HALO:HALO + ts, :] = val * jax.nn.sigmoid(gt)
        for r0 in range(0, ts, CONV_ROWS):
            starts = [r0 + g * ROW_GROUP + q for g in range(CONV_ROWS // ROW_GROUP) for q in range(ROW_STRIDE)]
            acc = [jnp.broadcast_to(conv_b_ref[s], (SUBLANES, LANES)) for _ in starts]
            for k in range(CONV_WIDTH):
                back = CONV_WIDTH - 1 - k
                w_k = conv_w_ref[s, k:k + 1, :]
                for n, start in enumerate(starts):
                    acc[n] = acc[n] + w_k * g_hist[s, _strided_rows(HALO + start - back), :]
            for n, start in enumerate(starts):
                conv_stage[s, _strided_rows(start), :] = acc[n]
    g_hist[:, 0:HALO, :] = g_hist[:, ts:ts + HALO, :]

    a = jnp.dot(h, w_in_ref[:, 0:WIDTH_POOL], preferred_element_type=F32)
    for s in range(N_POOL_SLABS):
        a_hist[s, HALO:HALO + ts, :] = a[:, s * LANES:(s + 1) * LANES]
    row_in_group = lax.broadcasted_iota(jnp.int32, (SUBLANES, LANES), 0) * ROW_STRIDE
    for gi, win in enumerate(POOL_WINDOWS):
        for r0 in range(0, ts, ROW_GROUP):
            for q in range(ROW_STRIDE):
                cur = a_hist[gi, _strided_rows(HALO + r0 + q), :]
                wsum = cur
                for back in range(1, win):
                    wsum = wsum + a_hist[gi, _strided_rows(HALO + r0 + q - back), :]
                pos = row_in_group + (j * ts + (r0 + q + 1))
                count = jnp.minimum(pos, win).astype(F32)
                pool_stage[gi, _strided_rows(r0 + q), :] = wsum / count - cur
    a_hist[:, 0:HALO, :] = a_hist[:, ts:ts + HALO, :]
    for gi in range(N_POOL_SLABS):
        lanes = slice(gi * LANES, (gi + 1) * LANES)
        y = jnp.dot(pool_stage[gi].astype(BF16), w_pool_ref[gi], preferred_element_type=F32)
        ycat[:, lanes] = (y * ls_pool_ref[:, lanes]).astype(BF16)

    z = jnp.dot(h, w_in_ref[:, OFF_Z:IN_WIDTH], preferred_element_type=F32)
    z = 0.5 * z * (1.0 + lax.erf(z * (2.0 ** -0.5)))
    u = z[:, 0:WIDTH_SGU]
    v = _layer_norm(z[:, WIDTH_SGU:], ln_sgu_g_ref[...], ln_sgu_b_ref[...]).astype(BF16)
    n_chunks = ts // CHUNK
    row = lax.broadcasted_iota(jnp.int32, (CHUNK, CHUNK), 0)
    col = lax.broadcasted_iota(jnp.int32, (CHUNK, CHUNK), 1)
    causal = col <= row
    for hd in range(N_SGU_HEADS):
        lanes = slice(hd * SGU_HEAD_DIM, (hd + 1) * SGU_HEAD_DIM)
        w_h = jnp.where(causal, w_sgu_ref[hd], 0.0).astype(BF16)
        v_h = jnp.concatenate([v[n * CHUNK:(n + 1) * CHUNK, lanes] for n in range(n_chunks)], axis=1)
        m_h = jnp.dot(w_h, v_h, preferred_element_type=F32)
        for n in range(n_chunks):
            rows = slice(n * CHUNK, (n + 1) * CHUNK)
            mixed = m_h[:, n * SGU_HEAD_DIM:(n + 1) * SGU_HEAD_DIM] + b_sgu_ref[:, lanes]
            ycat[rows, OFF_GATE + hd * SGU_HEAD_DIM:OFF_GATE + (hd + 1) * SGU_HEAD_DIM] = (
                u[rows, lanes] * mixed).astype(BF16)

    def out_proj(lo, hi):
        return jnp.dot(ycat[:, lo:hi], w_out_ref[lo:hi, :], preferred_element_type=F32)

    y = out_proj(0, OFF_VAL) + out_proj(OFF_GATE, D_MODEL)
    o_ref[...] = ALPHA * x_ref[...] + gate * y

    for r0 in range(0, ts, LN_ROWS):
        rows = slice(r0, r0 + LN_ROWS)
        pre = [conv_stage[s, rows, :] for s in range(N_CONV_SLABS)]
        tot = pre[0]
        for s in range(1, N_CONV_SLABS):
            tot = tot + pre[s]
        mu = jnp.sum(tot, axis=-1, keepdims=True) / WIDTH_CONV
        xc = [p - mu for p in pre]
        sq = xc[0] * xc[0]
        for s in range(1, N_CONV_SLABS):
            sq = sq + xc[s] * xc[s]
        rstd = lax.rsqrt(jnp.sum(sq, axis=-1, keepdims=True) / WIDTH_CONV + LN_EPS)
        for s in range(N_CONV_SLABS):
            hn = xc[s] * rstd * ln_conv_g_ref[s] + ln_conv_b_ref[s]
            ycat[rows, OFF_VAL + s * LANES:OFF_VAL + (s + 1) * LANES] = (hn * jax.nn.sigmoid(hn)).astype(BF16)

    o_ref[...] += gate * out_proj(OFF_VAL, OFF_GATE)
    for r0 in range(0, ts, LN_ROWS):
        rows = slice(r0, r0 + LN_ROWS)
        o_ref[rows, :] = _layer_norm(o_ref[rows, :], ln_g_ref[...], ln_b_ref[...])


def _mixer_call(x, mod, w_in, w_pool, ls_pool, conv_w, conv_b, ln_conv_g, ln_conv_b,
                w_sgu, b_sgu_rows, ln_sgu_g, ln_sgu_b, w_out, ln_g, ln_b):
    batch, seq, d = x.shape
    ts = MIX_ROWS
    const2 = lambda b, j: (0, 0)
    const3 = lambda b, j: (0, 0, 0)
    resident = pl.Buffered(1)
    row = lambda v: v.reshape(1, -1)
    slab_rows = lambda v: v.reshape(N_CONV_SLABS, 1, LANES)
    conv_w_slabs = conv_w.reshape(CONV_WIDTH, N_CONV_SLABS, LANES).transpose(1, 0, 2)
    return pl.pallas_call(
        _mixer_kernel,
        grid=(batch, seq // ts),
        in_specs=[
            pl.BlockSpec((None, ts, d), lambda b, j: (b, j, 0)),
            pl.BlockSpec((None, N_MOD, d), lambda b, j: (b, 0, 0)),
            pl.BlockSpec(w_in.shape, const2, pipeline_mode=resident),
            pl.BlockSpec(w_pool.shape, const3),
            pl.BlockSpec((1, WIDTH_POOL), const2),
            pl.BlockSpec((N_CONV_SLABS, CONV_WIDTH, LANES), const3),
            pl.BlockSpec((N_CONV_SLABS, 1, LANES), const3),
            pl.BlockSpec((N_CONV_SLABS, 1, LANES), const3),
            pl.BlockSpec((N_CONV_SLABS, 1, LANES), const3),
            pl.BlockSpec(w_sgu.shape, const3),
            pl.BlockSpec(b_sgu_rows.shape, const2),
            pl.BlockSpec((1, WIDTH_SGU), const2),
            pl.BlockSpec((1, WIDTH_SGU), const2),
            pl.BlockSpec(w_out.shape, const2, pipeline_mode=resident),
            pl.BlockSpec((1, d), const2),
            pl.BlockSpec((1, d), const2),
        ],
        out_specs=pl.BlockSpec((None, ts, d), lambda b, j: (b, j, 0)),
        out_shape=jax.ShapeDtypeStruct(x.shape, F32),
        scratch_shapes=[
            pltpu.VMEM((N_POOL_SLABS, HALO + ts, LANES), F32),
            pltpu.VMEM((N_CONV_SLABS, HALO + ts, LANES), F32),
            pltpu.VMEM((N_POOL_SLABS, ts, LANES), F32),
            pltpu.VMEM((N_CONV_SLABS, ts, LANES), F32),
            pltpu.VMEM((ts, d), BF16),
        ],
        compiler_params=pltpu.CompilerParams(
            dimension_semantics=("arbitrary", "arbitrary"), vmem_limit_bytes=VMEM_LIMIT_BYTES),
        name="mixer",
    )(x, mod, w_in, w_pool, row(ls_pool), conv_w_slabs, slab_rows(conv_b), slab_rows(ln_conv_g),
      slab_rows(ln_conv_b), w_sgu, b_sgu_rows, row(ln_sgu_g), row(ln_sgu_b), w_out, row(ln_g), row(ln_b))


def _ffn_kernel(x_ref, mod_ref, w1_ref, w2_ref, ln_g_ref, ln_b_ref, o_ref, h_ref):
    k = pl.program_id(1)
    last = pl.num_programs(1) - 1
    gate = mod_ref[5:6, :]

    def hidden():
        a = jnp.dot(h_ref[...], w1_ref[...].astype(BF16), preferred_element_type=F32)
        a = jnp.maximum(a, 0.0)
        return (a * a).astype(BF16)

    def down(a2):
        return jnp.dot(a2, w2_ref[...].astype(BF16), preferred_element_type=F32)

    @pl.when(k == 0)
    def _():
        h_ref[...] = (x_ref[...] * (1.0 + mod_ref[4:5, :]) + mod_ref[3:4, :]).astype(BF16)
        o_ref[...] = ALPHA * x_ref[...] + gate * down(hidden())

    @pl.when(jnp.logical_and(k > 0, k < last))
    def _():
        o_ref[...] += gate * down(hidden())

    @pl.when(k == last)
    def _():
        a2 = hidden()
        blocks = [slice(r0, r0 + FFN_TAIL_ROWS) for r0 in range(0, o_ref.shape[0], FFN_TAIL_ROWS)]

        def project(rows):
            o_ref[rows, :] += gate * down(a2[rows, :])

        project(blocks[0])
        for done, rows in zip(blocks, blocks[1:]):
            project(rows)
            o_ref[done, :] = _layer_norm(o_ref[done, :], ln_g_ref[...], ln_b_ref[...])
        o_ref[blocks[-1], :] = _layer_norm(o_ref[blocks[-1], :], ln_g_ref[...], ln_b_ref[...])


def _ffn_call(x, mod, w1_layers, w2_layers, layer, ln_g, ln_b):
    batch, seq, d = x.shape
    n_chunks = w1_layers.shape[2] // FFN_COLS
    tm = FFN_ROWS
    tiles_per_seq = seq // tm
    x_map = lambda i, k: (i // tiles_per_seq, i % tiles_per_seq, 0)
    return pl.pallas_call(
        _ffn_kernel,
        grid=(batch * tiles_per_seq, n_chunks),
        in_specs=[
            pl.BlockSpec((None, tm, d), x_map),
            pl.BlockSpec((None, N_MOD, d), lambda i, k: (i // tiles_per_seq, 0, 0)),
            pl.BlockSpec((None, d, FFN_COLS), lambda i, k: (layer, 0, k)),
            pl.BlockSpec((None, FFN_COLS, d), lambda i, k: (layer, k, 0)),
            pl.BlockSpec((1, d), lambda i, k: (0, 0)),
            pl.BlockSpec((1, d), lambda i, k: (0, 0)),
        ],
        out_specs=pl.BlockSpec((None, tm, d), x_map),
        out_shape=jax.ShapeDtypeStruct(x.shape, F32),
        scratch_shapes=[pltpu.VMEM((tm, d), BF16)],
        compiler_params=pltpu.CompilerParams(
            dimension_semantics=("arbitrary", "arbitrary"), vmem_limit_bytes=FFN_VMEM_LIMIT_BYTES),
        name="ffn",
    )(x, mod, w1_layers, w2_layers, ln_g.reshape(1, d), ln_b.reshape(1, d))


def kernel(x, c, w_mod, b_mod, w_in, w_pool, ls_pool, conv_w, conv_b, ln_conv_g, ln_conv_b,
           w_sgu, b_sgu, ln_sgu_g, ln_sgu_b, w_out, ln_mix_g, ln_mix_b, w_ff1, w_ff2, ln_ff_g, ln_ff_b):
    batch = x.shape[0]
    mod = _mod_call(c, w_mod, b_mod).reshape(DEPTH, batch, N_MOD, D_MODEL)
    for l in range(DEPTH):
        b_sgu_rows = jnp.repeat(b_sgu[l].T, SGU_HEAD_DIM, axis=1)
        w_in_l = _cast_call(w_in, l, _interleaved_in_block)
        w_out_l = _cast_call(w_out, l, lambda j: j)
        x = _mixer_call(x, mod[l], w_in_l, w_pool[l].astype(BF16), ls_pool[l],
                        conv_w[l], conv_b[l], ln_conv_g[l], ln_conv_b[l],
                        w_sgu[l], b_sgu_rows, ln_sgu_g[l], ln_sgu_b[l],
                        w_out_l, ln_mix_g[l], ln_mix_b[l])
        x = _ffn_call(x, mod[l], w_ff1, w_ff2, l, ln_ff_g[l], ln_ff_b[l])
    return x
```
